```python
import math
import jax
import jax.numpy as jnp
from jax import lax
import numpy as np

D_MODEL = 1024
BATCH = 8
SEQ = 2048
DEPTH = 4
DEC_BATCH = 128
DEC_SEQ = 8
PAST_LEN = 8192
PAGE_SIZE = 128

N_EVEN = (DEPTH + 1) // 2
N_ODD = DEPTH // 2
H_A = 4
DK_A = 64
DV_A = 128
RET_THETA = 10000.0
H_B = 4
Q_LORA = 384
KV_LORA = 256
NOPE_B = 128
ROPE_B = 64
V_B = 128
H_C = 4
KV_C = 2
REP_C = H_C // KV_C
D_C = 64
ROT_C = D_C // 4
H_D = 4
DK_D = 64
DV_D = 128
GK_RANK = 16
GATE_NORM = 16.0
D_FF = 4 * D_MODEL
ROPE_THETA = 500000.0
ALPHA = (2 * DEPTH) ** 0.25
BETA = (8 * DEPTH) ** -0.25
LIN_CHUNK = 64
Q_BLOCK = 128
EPS = 1e-5
EVEN_SPLITS = (H_A * DK_A, H_A * DK_A, H_A * DV_A, H_A * DV_A, Q_LORA, KV_LORA, ROPE_B)
ODD_SPLITS = (H_C * 2 * D_C, KV_C * 2 * D_C, KV_C * 2 * D_C, H_D * DK_D, H_D * DK_D, H_D * DV_D, H_D * DV_D, GK_RANK)
IN_EVEN = sum(EVEN_SPLITS)
IN_ODD = sum(ODD_SPLITS)
MIX_EVEN = H_A * DV_A + H_B * V_B
MIX_ODD = H_C * 2 * D_C + H_D * DV_D

kernel_name = 'hybrid_retmla_diffgla_decode_step'


def _split(h, sizes):
    return jnp.split(h, [int(i) for i in np.cumsum(sizes)[:-1]], axis=-1)


def _layernorm(x, g, b):
    xf = x.astype(jnp.float32)
    mu = jnp.mean(xf, -1, keepdims=True)
    var = jnp.mean(jnp.square(xf - mu), -1, keepdims=True)
    return ((xf - mu) * lax.rsqrt(var + EPS) * g + b).astype(x.dtype)


def _rmsnorm(x, g=None):
    xf = x.astype(jnp.float32)
    y = xf * lax.rsqrt(jnp.mean(jnp.square(xf), -1, keepdims=True) + EPS)
    if g is not None:
        y = y * g
    return y.astype(x.dtype)


def _rope(x, pos, theta, rot):
    half = rot // 2
    inv = theta ** (-jnp.arange(half, dtype=jnp.float32) / half)
    ang = pos.astype(jnp.float32)[:, None] * inv[None, :]
    shp = (1, pos.shape[0]) + (1,) * (x.ndim - 3) + (half,)
    cos = jnp.cos(ang).reshape(shp)
    sin = jnp.sin(ang).reshape(shp)
    xf = x.astype(jnp.float32)
    x1, x2 = xf[..., :half], xf[..., half:rot]
    parts = [x1 * cos - x2 * sin, x2 * cos + x1 * sin]
    if rot < x.shape[-1]:
        parts.append(xf[..., rot:])
    return jnp.concatenate(parts, -1).astype(x.dtype)


def _chunked_linear_attn(q, k, v, log_a, s0):
    b_, t_, h_, k_ = q.shape
    v_ = v.shape[-1]
    c = math.gcd(t_, LIN_CHUNK)
    n = t_ // c

    def blocks(a):
        return a.reshape((b_, n, c) + a.shape[2:]).swapaxes(0, 1)

    causal = jnp.tril(jnp.ones((c, c), bool))[None, :, :, None, None]

    def step(s, inp):
        qc, kc, vc, ac = inp
        cum = jnp.cumsum(ac.astype(jnp.float32), axis=1)
        rel = jnp.where(causal, cum[:, :, None] - cum[:, None, :], -jnp.inf)
        dec = jnp.exp(rel)
        scores = jnp.einsum('bihk,bjhk,bijhk->bhij', qc, kc, dec)
        intra = jnp.einsum('bhij,bjhv->bihv', scores, vc)
        inter = jnp.einsum('bihk,bhkv->bihv', qc * jnp.exp(cum), s)
        last = cum[:, -1]
        s_new = jnp.exp(last)[..., None] * s + jnp.einsum('bjhk,bjhv->bhkv', kc * jnp.exp(last[:, None] - cum), vc)
        return s_new, inter + intra

    s_fin, o = lax.scan(step, s0.astype(jnp.float32), (blocks(q), blocks(k), blocks(v), blocks(log_a)))
    return o.swapaxes(0, 1).reshape(b_, t_, h_, v_).astype(v.dtype), s_fin


def _sweep(fn, qs, qpos):
    t_ = qpos.shape[0]
    qb = math.gcd(t_, Q_BLOCK)
    n = t_ // qb
    blks = tuple(a.reshape((a.shape[0], n, qb) + a.shape[2:]).swapaxes(0, 1) for a in qs)
    out = lax.map(lambda args: fn(args[0], args[1]), (blks, qpos.reshape(n, qb)))
    return out.swapaxes(0, 1).reshape((out.shape[1], t_) + out.shape[3:])


def _mla_core(q_lat, q_rope, segs, scale):
    scores = []
    for c, r, m in segs:
        s = (jnp.einsum('bqhc,bkc->bhqk', q_lat, c) + jnp.einsum('bqhr,bkr->bhqk', q_rope, r)).astype(jnp.float32) * scale
        scores.append(s if m is None else jnp.where(m, s, -jnp.inf))
    p = jax.nn.softmax(jnp.concatenate(scores, -1), axis=-1)
    out, off = 0.0, 0
    for c, _, _ in segs:
        n = c.shape[1]
        out = out + jnp.einsum('bhqk,bkc->bqhc', p[..., off:off + n].astype(c.dtype), c)
        off += n
    return out


def _diff_core(q, segs, lam, scale):
    scores = []
    for k, _, m in segs:
        s = jnp.einsum('bqgrsd,bkgsd->bsgrqk', q, k).astype(jnp.float32) * scale
        scores.append(s if m is None else jnp.where(m, s, -jnp.inf))
    p = jax.nn.softmax(jnp.concatenate(scores, -1), axis=-1)
    a = p[:, 0] - lam * p[:, 1]
    out, off = 0.0, 0
    for k, v, _ in segs:
        n = k.shape[1]
        out = out + jnp.einsum('bgrqk,bkgv->bqgrv', a[..., off:off + n].astype(v.dtype), v)
        off += n
    return out


def _mlp(x, w1, w2):
    return jnp.square(jax.nn.relu(x @ w1)) @ w2


def _even_mixer(x, pos, w_in, g_q, g_kv, w_qu, w_ku, w_vu, w_out, s0, past):
    b_, t_, _ = x.shape
    qa, ka, va, ga, cq, ckv, kr = _split(x @ w_in, EVEN_SPLITS)
    qa = _rope(qa.reshape(b_, t_, H_A, DK_A), pos, RET_THETA, DK_A)
    ka = _rope(ka.reshape(b_, t_, H_A, DK_A), pos, RET_THETA, DK_A) * (DK_A ** -0.5)
    va = va.reshape(b_, t_, H_A, DV_A)
    log_g = jnp.log1p(-jnp.exp2(-5.0 - jnp.arange(H_A, dtype=jnp.float32)))
    log_g = jnp.broadcast_to(log_g[:, None], (b_, t_, H_A, DK_A))
    o_a, s_a = _chunked_linear_attn(qa, ka, va, log_g, s0)
    o_a = (_rmsnorm(o_a) * jax.nn.silu(ga.reshape(b_, t_, H_A, DV_A))).reshape(b_, t_, H_A * DV_A)
    q = jnp.einsum('btc,chd->bthd', _rmsnorm(cq, g_q), w_qu)
    q_rope = _rope(q[..., NOPE_B:], pos, ROPE_THETA, ROPE_B)
    q_lat = jnp.einsum('bthn,chn->bthc', q[..., :NOPE_B], w_ku)
    ckv = _rmsnorm(ckv, g_kv)
    kr = _rope(kr, pos, ROPE_THETA, ROPE_B)
    scale = (NOPE_B + ROPE_B) ** -0.5
    if past is None:
        lat = _sweep(lambda qs, qp: _mla_core(qs[0], qs[1], [(ckv, kr, qp[:, None] >= pos[None, :])], scale), (q_lat, q_rope), pos)
    else:
        ckv_p, kr_p = past
        lat = _mla_core(q_lat, q_rope, [(ckv_p, kr_p, None), (ckv, kr, pos[:, None] >= pos[None, :])], scale)
    o_b = jnp.einsum('bthc,chv->bthv', lat, w_vu).reshape(b_, t_, H_B * V_B)
    y = jnp.concatenate([o_a, o_b], -1) @ w_out
    return y, s_a, ckv, kr


def _odd_mixer(x, pos, layer, w_in, lq1, lk1, lq2, lk2, g_sub, w_gu, b_gu, g_gn, w_out, s0, past):
    b_, t_, _ = x.shape
    qc, kc, vc, qd, kd, vd, gd, lr = _split(x @ w_in, ODD_SPLITS)
    qc = _rope(qc.reshape(b_, t_, KV_C, REP_C, 2, D_C), pos, ROPE_THETA, ROT_C)
    kc = _rope(kc.reshape(b_, t_, KV_C, 2, D_C), pos, ROPE_THETA, ROT_C)
    vc = vc.reshape(b_, t_, KV_C, 2 * D_C)
    lam_init = 0.8 - 0.6 * math.exp(-0.3 * layer)
    lam = (jnp.exp(jnp.sum(lq1.astype(jnp.float32) * lk1)) - jnp.exp(jnp.sum(lq2.astype(jnp.float32) * lk2)) + lam_init)
    scale = D_C ** -0.5
    if past is None:
        o_c = _sweep(lambda qs, qp: _diff_core(qs[0], [(kc, vc, qp[:, None] >= pos[None, :])], lam, scale), (qc,), pos)
    else:
        k_p, v_p = past
        k_p = k_p.reshape(k_p.shape[:3] + (2, D_C))
        o_c = _diff_core(qc, [(k_p, v_p, None), (kc, vc, pos[:, None] >= pos[None, :])], lam, scale)
    o_c = (_rmsnorm(o_c, g_sub) * (1.0 - lam_init)).reshape(b_, t_, H_C * 2 * D_C)
    qd = qd.reshape(b_, t_, H_D, DK_D) * (DK_D ** -0.5)
    kd = kd.reshape(b_, t_, H_D, DK_D)
    vd = vd.reshape(b_, t_, H_D, DV_D)
    log_a = (jax.nn.log_sigmoid((lr @ w_gu + b_gu).astype(jnp.float32)) / GATE_NORM).reshape(b_, t_, H_D, DK_D)
    o_d, s_d = _chunked_linear_attn(qd, kd, vd, log_a, s0)
    o_d = (_rmsnorm(o_d, g_gn) * jax.nn.silu(gd.reshape(b_, t_, H_D, DV_D))).reshape(b_, t_, H_D * DV_D)
    y = jnp.concatenate([o_c, o_d], -1) @ w_out
    return y, kc.reshape(b_, t_, KV_C, 2 * D_C), vc, s_d


def setup_inputs(seed: int = 0) -> dict:
    key = jax.random.key(seed)
    ks = iter(jax.random.split(key, 64))

    def nrm(shape, scale=1.0):
        return jax.random.normal(next(ks), shape, jnp.float32) * scale

    n_pages = PAST_LEN // PAGE_SIZE
    n_used = DEC_BATCH * n_pages
    n_pool = n_used + max(1, n_used // 4)
    page_table = jax.random.permutation(next(ks), n_pool)[:n_used].reshape(DEC_BATCH, n_pages).astype(jnp.int32)
    return {
        'x_prompt': nrm((BATCH, SEQ, D_MODEL)),
        'x_sample': nrm((DEC_BATCH, DEC_SEQ, D_MODEL)),
        'state_ret': nrm((N_EVEN, DEC_BATCH, H_A, DK_A, DV_A), 0.5),
        'cache_mla_ckv': nrm((N_EVEN, n_pool, PAGE_SIZE, KV_LORA)),
        'cache_mla_krope': nrm((N_EVEN, n_pool, PAGE_SIZE, ROPE_B)),
        'cache_diff_k': nrm((N_ODD, n_pool, PAGE_SIZE, KV_C, 2 * D_C)),
        'cache_diff_v': nrm((N_ODD, n_pool, PAGE_SIZE, KV_C, 2 * D_C)),
        'state_gla': nrm((N_ODD, DEC_BATCH, H_D, DK_D, DV_D), 0.5),
        'page_table': page_table,
        'w_in_even': nrm((N_EVEN, D_MODEL, IN_EVEN), D_MODEL ** -0.5),
        'g_q_lora': 1.0 + nrm((N_EVEN, Q_LORA), 0.02),
        'g_kv_lora': 1.0 + nrm((N_EVEN, KV_LORA), 0.02),
        'w_q_up': nrm((N_EVEN, Q_LORA, H_B, NOPE_B + ROPE_B), Q_LORA ** -0.5),
        'w_k_up': nrm((N_EVEN, KV_LORA, H_B, NOPE_B), KV_LORA ** -0.5),
        'w_v_up': nrm((N_EVEN, KV_LORA, H_B, V_B), KV_LORA ** -0.5),
        'w_out_even': nrm((N_EVEN, MIX_EVEN, D_MODEL), BETA * MIX_EVEN ** -0.5),
        'w_in_odd': nrm((N_ODD, D_MODEL, IN_ODD), D_MODEL ** -0.5),
        'lam_q1': nrm((N_ODD, D_C), 0.1),
        'lam_k1': nrm((N_ODD, D_C), 0.1),
        'lam_q2': nrm((N_ODD, D_C), 0.1),
        'lam_k2': nrm((N_ODD, D_C), 0.1),
        'g_subln': 1.0 + nrm((N_ODD, 2 * D_C), 0.02),
        'w_gate_up': nrm((N_ODD, GK_RANK, H_D * DK_D), GK_RANK ** -0.5),
        'b_gate_up': nrm((N_ODD, H_D * DK_D), 0.1),
        'g_gla_norm': 1.0 + nrm((N_ODD, DV_D), 0.02),
        'w_out_odd': nrm((N_ODD, MIX_ODD, D_MODEL), BETA * MIX_ODD ** -0.5),
        'ln1_g': 1.0 + nrm((DEPTH, D_MODEL), 0.02),
        'ln1_b': nrm((DEPTH, D_MODEL), 0.01),
        'ln2_g': 1.0 + nrm((DEPTH, D_MODEL), 0.02),
        'ln2_b': nrm((DEPTH, D_MODEL), 0.01),
        'w_ff1': nrm((DEPTH, D_MODEL, D_FF), D_MODEL ** -0.5),
        'w_ff2': nrm((DEPTH, D_FF, D_MODEL), BETA * D_FF ** -0.5),
    }


def reference(x_prompt, x_sample, state_ret, cache_mla_ckv, cache_mla_krope, cache_diff_k, cache_diff_v, state_gla,
              page_table, w_in_even, g_q_lora, g_kv_lora, w_q_up, w_k_up, w_v_up, w_out_even, w_in_odd,
              lam_q1, lam_k1, lam_q2, lam_k2, g_subln, w_gate_up, b_gate_up, g_gla_norm, w_out_odd,
              ln1_g, ln1_b, ln2_g, ln2_b, w_ff1, w_ff2):
    bp, tp = x_prompt.shape[0], x_prompt.shape[1]
    bs, ts = x_sample.shape[0], x_sample.shape[1]
    past_len = page_table.shape[1] * cache_mla_ckv.shape[2]
    pos_p = jnp.arange(tp, dtype=jnp.int32)
    pos_s = past_len + jnp.arange(ts, dtype=jnp.int32)

    def gather(cache, j):
        rows = cache[j, page_table]
        return rows.reshape((bs, past_len) + rows.shape[3:])

    xp, xs = x_prompt, x_sample
    p_ret, p_ckv, p_kr, p_dk, p_dv, p_gla = [], [], [], [], [], []
    s_ret, s_ckv, s_kr, s_dk, s_dv, s_gla = [], [], [], [], [], []
    for l in range(DEPTH):
        j = l // 2
        if l % 2 == 0:
            w = (w_in_even[j], g_q_lora[j], g_kv_lora[j], w_q_up[j], w_k_up[j], w_v_up[j], w_out_even[j])
            yp, st, ckv, kr = _even_mixer(xp, pos_p, *w, jnp.zeros((bp, H_A, DK_A, DV_A), jnp.float32), None)
            p_ret.append(st)
            p_ckv.append(ckv)
            p_kr.append(kr)
            ys, st, ckv, kr = _even_mixer(xs, pos_s, *w, state_ret[j], (gather(cache_mla_ckv, j), gather(cache_mla_krope, j)))
            s_ret.append(st)
            s_ckv.append(ckv)
            s_kr.append(kr)
        else:
            w = (w_in_odd[j], lam_q1[j], lam_k1[j], lam_q2[j], lam_k2[j], g_subln[j], w_gate_up[j], b_gate_up[j],
                 g_gla_norm[j], w_out_odd[j])
            yp, kk, vv, st = _odd_mixer(xp, pos_p, l, *w, jnp.zeros((bp, H_D, DK_D, DV_D), jnp.float32), None)
            p_dk.append(kk)
            p_dv.append(vv)
            p_gla.append(st)
            ys, kk, vv, st = _odd_mixer(xs, pos_s, l, *w, state_gla[j], (gather(cache_diff_k, j), gather(cache_diff_v, j)))
            s_dk.append(kk)
            s_dv.append(vv)
            s_gla.append(st)
        xp = _layernorm(ALPHA * xp + yp, ln1_g[l], ln1_b[l])
        xp = _layernorm(ALPHA * xp + _mlp(xp, w_ff1[l], w_ff2[l]), ln2_g[l], ln2_b[l])
        xs = _layernorm(ALPHA * xs + ys, ln1_g[l], ln1_b[l])
        xs = _layernorm(ALPHA * xs + _mlp(xs, w_ff1[l], w_ff2[l]), ln2_g[l], ln2_b[l])
    return (xp, xs,
            jnp.stack(p_ret), jnp.stack(p_ckv), jnp.stack(p_kr), jnp.stack(p_dk), jnp.stack(p_dv), jnp.stack(p_gla),
            jnp.stack(s_ret), jnp.stack(s_ckv), jnp.stack(s_kr), jnp.stack(s_dk), jnp.stack(s_dv), jnp.stack(s_gla))
```

```python
import functools
import math

import numpy as np
import jax
import jax.numpy as jnp
from jax import lax
from jax.experimental import pallas as pl
from jax.experimental.pallas import tpu as pltpu

F32 = jnp.float32
BF16 = jnp.bfloat16

D_MODEL = 1024
DEPTH = 4
H_A, DK_A, DV_A = 4, 64, 128
RET_THETA = 10000.0
H_B, Q_LORA, KV_LORA, NOPE_B, ROPE_B, V_B = 4, 384, 256, 128, 64, 128
H_C, KV_C, D_C = 4, 2, 64
REP_C = H_C // KV_C
ROT_C = D_C // 4
H_D, DK_D, DV_D = 4, 64, 128
GK_RANK = 16
GATE_NORM = 16.0
D_FF = 4 * D_MODEL
ROPE_THETA = 500000.0
ALPHA = (2 * DEPTH) ** 0.25
EPS = 1e-5
EVEN_SPLITS = (H_A * DK_A, H_A * DK_A, H_A * DV_A, H_A * DV_A, Q_LORA, KV_LORA, ROPE_B)
ODD_SPLITS = (H_C * 2 * D_C, KV_C * 2 * D_C, KV_C * 2 * D_C, H_D * DK_D, H_D * DK_D, H_D * DV_D, H_D * DV_D, GK_RANK)

LANES = 128
LIN_TILE = 64
NEG_BIG = -1e30
TQ_PREF = 512
VMEM_LIMIT = 52 * 1024 * 1024


def _dot(a, b):
    return jnp.dot(a, b, preferred_element_type=F32)


def _dot_nt(a, b):
    return lax.dot_general(a, b, (((1,), (1,)), ((), ())), preferred_element_type=F32)


def _dot_tn(a, b):
    return lax.dot_general(a, b, (((0,), (0,)), ((), ())), preferred_element_type=F32)


def _bf(x):
    return x.astype(BF16)


def _rms(x):
    return x * lax.rsqrt(jnp.mean(x * x, axis=-1, keepdims=True) + EPS)


def _silu(x):
    return x / (1.0 + jnp.exp(-x))


def _tile(n, pref):
    t = min(n, pref)
    while n % t:
        t -= 8
    return t


def _params(*sem):
    return pltpu.CompilerParams(dimension_semantics=sem, vmem_limit_bytes=VMEM_LIMIT)


def _rope_table(pos, theta, rot, group, n_groups):
    half = rot // 2
    inv = theta ** (-jnp.arange(half, dtype=F32) / half)
    ang = pos.astype(F32)[:, None] * inv[None, :]
    cos, sin = jnp.cos(ang), jnp.sin(ang)
    n = pos.shape[0]
    one = jnp.ones((n, group - rot), F32)
    zh = jnp.zeros((n, half), F32)
    zr = jnp.zeros((n, group - rot), F32)
    c = jnp.concatenate([cos, cos, one], -1)
    s1 = jnp.concatenate([-sin, zh, zr], -1)
    s2 = jnp.concatenate([zh, sin, zr], -1)
    pad = jnp.zeros((n, LANES - n_groups * group), F32)
    cat = lambda t: jnp.concatenate([t] * n_groups + [pad], -1)
    return jnp.concatenate([cat(c), cat(s1), cat(s2)], -1)


def _rope_slab(x, tab, half):
    c, s1, s2 = tab[:, 0:LANES], tab[:, LANES:2 * LANES], tab[:, 2 * LANES:3 * LANES]
    return x * c + pltpu.roll(x, LANES - half, 1) * s1 + pltpu.roll(x, half, 1) * s2


def _proj_even_kernel(x_ref, w_ref, taba_ref, tabb_ref, gq_ref, gkv_ref, wqu_ref, wku_ref,
                      qa_ref, ka_ref, va_ref, ga_ref, qm_ref, ckv_ref, kr_ref, kvm_ref):
    h = _dot(_bf(x_ref[...]), w_ref[...])
    taba = taba_ref[...]
    tabb = tabb_ref[...]
    for j in range(2):
        sl = slice(j * LANES, (j + 1) * LANES)
        qa_ref[:, sl] = _rope_slab(h[:, j * LANES:(j + 1) * LANES], taba, DK_A // 2)
        ka_ref[:, sl] = _rope_slab(h[:, 256 + j * LANES:256 + (j + 1) * LANES], taba, DK_A // 2) * (DK_A ** -0.5)
    va_ref[...] = _bf(h[:, 512:1024])
    ga_ref[...] = h[:, 1024:1536]
    cqn = _rms(h[:, 1536:1920]) * gq_ref[...]
    ckvn = _rms(h[:, 1920:2176]) * gkv_ref[...]
    kr = _rope_slab(h[:, 2176:2304], tabb, ROPE_B // 2)
    ckv_ref[...] = ckvn
    kr_ref[...] = kr[:, 0:ROPE_B]
    kvm_ref[:, 0:KV_LORA] = _bf(ckvn)
    kvm_ref[:, KV_LORA:KV_LORA + LANES] = _bf(kr)
    q = _dot(_bf(cqn), wqu_ref[...])
    for hd in range(H_B):
        q_lat = _dot(_bf(q[:, hd * NOPE_B:(hd + 1) * NOPE_B]), wku_ref[hd])
        base = hd * (KV_LORA + LANES)
        qm_ref[:, base:base + KV_LORA] = _bf(q_lat)
        qr = _rope_slab(q[:, H_B * NOPE_B + hd * LANES:H_B * NOPE_B + (hd + 1) * LANES], tabb, ROPE_B // 2)
        qm_ref[:, base + KV_LORA:base + KV_LORA + LANES] = _bf(qr)


def _tab_index(tm, t_len, n_prompt):
    per = t_len // tm
    n_pt = n_prompt // tm
    return lambda i: (jnp.where(i < n_pt, i % per, per + i - n_pt), 0)


def _proj_even(x, w_in, taba, tabb, g_q, g_kv, w_qu, w_ku, t_len, n_prompt, tm):
    n = x.shape[0]
    row = lambda w: pl.BlockSpec((tm, w), lambda i: (i, 0))
    full = lambda a: pl.BlockSpec(a.shape, lambda i: (0,) * a.ndim)
    tix = _tab_index(tm, t_len, n_prompt)
    qw = H_B * (KV_LORA + LANES)
    outs = [(256, F32), (256, F32), (512, BF16), (512, F32), (qw, BF16), (KV_LORA, F32), (ROPE_B, F32),
            (KV_LORA + LANES, BF16)]
    return pl.pallas_call(
        _proj_even_kernel,
        grid=(n // tm,),
        in_specs=[row(D_MODEL), full(w_in), pl.BlockSpec((tm, 3 * LANES), tix), pl.BlockSpec((tm, 3 * LANES), tix),
                  full(g_q), full(g_kv), full(w_qu), full(w_ku)],
        out_specs=[row(w) for w, _ in outs],
        out_shape=[jax.ShapeDtypeStruct((n, w), d) for w, d in outs],
        compiler_params=_params("parallel"),
        name="proj_even",
    )(x, w_in, taba, tabb, g_q, g_kv, w_qu, w_ku)


def _proj_odd_kernel(x_ref, w_ref, tabc_ref, wgu_ref, bgu_ref,
                     qdm_ref, kc_ref, vc_ref, kcm_ref, vcm_ref, qd_ref, kd_ref, vd_ref, gd_ref, la_ref):
    h = _dot(_bf(x_ref[...]), w_ref[...])
    tabc = tabc_ref[...]
    lane = lax.broadcasted_iota(jnp.int32, (1, LANES), 1)
    lo = (lane < D_C).astype(F32)
    hi = 1.0 - lo
    for j in range(4):
        qs = _rope_slab(h[:, j * LANES:(j + 1) * LANES], tabc, ROT_C // 2)
        qdm_ref[:, (2 * j) * LANES:(2 * j + 1) * LANES] = _bf(qs * lo)
        qdm_ref[:, (2 * j + 1) * LANES:(2 * j + 2) * LANES] = _bf(qs * hi)
    for j in range(2):
        sl = slice(j * LANES, (j + 1) * LANES)
        ks = _rope_slab(h[:, 512 + j * LANES:512 + (j + 1) * LANES], tabc, ROT_C // 2)
        kc_ref[:, sl] = ks
        kcm_ref[:, sl] = _bf(ks)
    vc = h[:, 768:1024]
    vc_ref[...] = vc
    vcm_ref[...] = _bf(vc)
    qd_ref[...] = h[:, 1024:1280] * (DK_D ** -0.5)
    kd_ref[...] = h[:, 1280:1536]
    vd_ref[...] = _bf(h[:, 1536:2048])
    gd_ref[...] = h[:, 2048:2560]
    z = _dot(_bf(h[:, 2560:2688]), wgu_ref[...]) + bgu_ref[...]
    la_ref[...] = (jnp.minimum(z, 0.0) - jnp.log1p(jnp.exp(-jnp.abs(z)))) * (1.0 / GATE_NORM)


def _proj_odd(x, w_in, tabc, w_gu, b_gu, t_len, n_prompt, tm):
    n = x.shape[0]
    row = lambda w: pl.BlockSpec((tm, w), lambda i: (i, 0))
    full = lambda a: pl.BlockSpec(a.shape, lambda i: (0,) * a.ndim)
    tix = _tab_index(tm, t_len, n_prompt)
    outs = [(1024, BF16), (256, F32), (256, F32), (256, BF16), (256, BF16), (256, F32), (256, F32), (512, BF16),
            (512, F32), (256, F32)]
    return pl.pallas_call(
        _proj_odd_kernel,
        grid=(n // tm,),
        in_specs=[row(D_MODEL), full(w_in), pl.BlockSpec((tm, 3 * LANES), tix), full(w_gu), full(b_gu)],
        out_specs=[row(w) for w, _ in outs],
        out_shape=[jax.ShapeDtypeStruct((n, w), d) for w, d in outs],
        compiler_params=_params("parallel"),
        name="proj_odd",
    )(x, w_in, tabc, w_gu, b_gu)


def _gla_consts(c):
    n = LIN_TILE
    i = np.arange(n)
    same = (i[:, None] // c) == (i[None, :] // c)
    mats = [same & (i[None, :] <= i[:, None]),
            same & (i[None, :] > i[:, None])]
    levels = []
    s = c // 2
    while s >= 1:
        levels.append(s)
        s //= 2
    mq, nk, masks = [], [], []
    for s in levels:
        blk, off = i // (2 * s), i % (2 * s)
        second = off >= s
        mid = blk * 2 * s + s
        mq.append(second[:, None] & (i[None, :] >= mid[:, None]) & (i[None, :] <= i[:, None]))
        nk.append((~second)[:, None] & (i[None, :] > i[:, None]) & (i[None, :] < mid[:, None]))
        masks.append(second[:, None] & (~second)[None, :] & (blk[:, None] == blk[None, :]))
    masks.append(i[:, None] == i[None, :])
    mall = np.concatenate(mats + mq + nk, 0).astype(np.float32)
    return jnp.asarray(mall, BF16), jnp.asarray(np.stack(masks).astype(np.float32)), len(levels)


def _ret_consts(c):
    n = LIN_TILE
    i = np.arange(n)
    log_g = np.log1p(-np.exp2(-5.0 - np.arange(H_A, dtype=np.float64)))
    same = (i[:, None] // c) == (i[None, :] // c)
    causal = same & (i[:, None] >= i[None, :])
    dmat = np.where(causal[None], np.exp(log_g[:, None, None] * (i[:, None] - i[None, :])[None]), 0.0)
    eq = np.exp(log_g[None, :] * ((i % c) + 1)[:, None])
    ek = np.exp(log_g[None, :] * (c - 1 - (i % c))[:, None])
    rep = lambda a: np.repeat(a, DK_A, axis=1)
    tabs = np.concatenate([rep(eq), rep(ek)], 1).astype(np.float32)
    glast = [float(np.exp(log_g[h] * c)) for h in range(H_A)]
    return jnp.asarray(dmat.astype(np.float32)), jnp.asarray(tabs), glast


def _split3(x):
    hi = _bf(x)
    r1 = x - hi.astype(F32)
    mid = _bf(r1)
    lo = _bf(r1 - mid.astype(F32))
    return hi, mid, lo


def _lin_tile(q, k, v, gate, la, consts, state_get, state_put, *, gla, c, dk, dv, gnorm, glast):
    n_heads = q.shape[1] // dk
    n_sub = LIN_TILE // c
    if gla:
        mall, masks, nl = consts
        hi, mid, lo = _split3(la)
        res = _dot(mall, hi) + _dot(mall, mid) + _dot(mall, lo)
        cum = res[0:LIN_TILE]
        qe = q * jnp.exp(cum)
        kk = k * jnp.exp(res[LIN_TILE:2 * LIN_TILE])
        qs = [q * jnp.exp(res[(2 + l) * LIN_TILE:(3 + l) * LIN_TILE]) for l in range(nl)] + [q]
        ks = [k * jnp.exp(res[(2 + nl + l) * LIN_TILE:(3 + nl + l) * LIN_TILE]) for l in range(nl)] + [k]
    else:
        dmat, tabs = consts
        qe = q * tabs[:, 0:n_heads * dk]
        kk = k * tabs[:, n_heads * dk:2 * n_heads * dk]
    v32 = v.astype(F32) if n_sub > 1 else None
    eye = None
    outs = []
    for h in range(n_heads):
        ksl = slice(h * dk, (h + 1) * dk)
        vsl = slice(h * dv, (h + 1) * dv)
        if gla:
            scores = masks[0] * _dot_nt(_bf(qs[0][:, ksl]), _bf(ks[0][:, ksl]))
            for l in range(1, nl + 1):
                scores = scores + masks[l] * _dot_nt(_bf(qs[l][:, ksl]), _bf(ks[l][:, ksl]))
        else:
            scores = dmat[h] * _dot_nt(_bf(q[:, ksl]), _bf(k[:, ksl]))
        o = _dot(_bf(scores), v[:, vsl])
        inter = []
        for u in range(n_sub):
            rs = slice(u * c, (u + 1) * c)
            s_old = state_get(u, h)
            inter.append(_dot(_bf(qe[rs, ksl]), _bf(s_old)))
            v_u = v[:, vsl] if n_sub == 1 else _bf(v32[rs, vsl])
            upd = _dot_tn(_bf(kk[rs, ksl]), v_u)
            if gla:
                if eye is None:
                    ii = lax.broadcasted_iota(jnp.int32, (dk, dk), 0)
                    jj = lax.broadcasted_iota(jnp.int32, (dk, dk), 1)
                    eye = (ii == jj).astype(F32)
                last = jnp.exp(cum[(u + 1) * c - 1:(u + 1) * c, ksl])
                dec = jnp.sum(eye * last, axis=1, keepdims=True)
            else:
                dec = glast[h]
            state_put(u, h, dec * s_old + upd)
        o = o + (inter[0] if n_sub == 1 else jnp.concatenate(inter, 0))
        o = _rms(o)
        if gnorm is not None:
            o = o * gnorm
        outs.append(o * _silu(gate[:, vsl]))
    return outs


def _lin_kernel(*refs, gla, c, seq, n_tiles, glast):
    it = iter(refs)
    q_ref, k_ref, v_ref, g_ref = next(it), next(it), next(it), next(it)
    la_ref = next(it) if gla else None
    gn_ref = next(it) if gla else None
    c1_ref, c2_ref = next(it), next(it)
    s_in_ref = None if seq else next(it)
    o_ref, s_out_ref = next(it), next(it)
    s_scr = next(it) if seq else None
    n_heads, dk, dv = H_D, DK_D, DV_D
    n_sub = LIN_TILE // c
    consts = (c1_ref[...], c2_ref[...], int(round(math.log2(c)))) if gla else (c1_ref[...], c2_ref[...])
    gnorm = gn_ref[...] if gla else None

    if seq:
        @pl.when(pl.program_id(1) == 0)
        def _():
            s_scr[...] = jnp.zeros_like(s_scr)

    for t in range(n_tiles):
        rows = slice(t * LIN_TILE, (t + 1) * LIN_TILE)
        if seq:
            get = lambda u, h: s_scr[h]

            def put(u, h, s):
                s_scr[h] = s
        else:
            get = lambda u, h, t=t: s_in_ref[t * n_sub + u, h]

            def put(u, h, s, t=t):
                s_out_ref[t * n_sub + u, h] = s
        outs = _lin_tile(q_ref[rows, :], k_ref[rows, :], v_ref[rows, :], g_ref[rows, :],
                         la_ref[rows, :] if gla else None, consts, get, put,
                         gla=gla, c=c, dk=dk, dv=dv, gnorm=gnorm, glast=glast)
        for h in range(n_heads):
            o_ref[rows, h * dv:(h + 1) * dv] = _bf(outs[h])

    if seq:
        @pl.when(pl.program_id(1) == pl.num_programs(1) - 1)
        def _():
            s_out_ref[0] = s_scr[...]


def _lin_attn(q, k, v, gate, la, gnorm, state, *, gla, n_prompt, t_len, b_prompt, c_s):
    n = q.shape[0]
    n_s = n - n_prompt
    res = []
    for seq in (True, False):
        c = LIN_TILE if seq else c_s
        if gla:
            c1, c2, _ = _gla_consts(c)
            glast = None
        else:
            c1, c2, glast = _ret_consts(c)
        rows = _tile(t_len if seq else n_s, 256)
        n_tiles = rows // LIN_TILE
        full = lambda a: pl.BlockSpec(a.shape, lambda *_: (0,) * a.ndim)
        if seq:
            per = t_len // rows
            grid = (b_prompt, per)
            rix = lambda b, i: (b * per + i, 0)
            s_shape = (b_prompt, H_D, DK_D, DV_D)
            s_spec = pl.BlockSpec((1, H_D, DK_D, DV_D), lambda b, i: (b, 0, 0, 0))
            sem = ("parallel", "arbitrary")
            scratch = [pltpu.VMEM((H_D, DK_D, DV_D), F32)]
        else:
            off = n_prompt // rows
            grid = (n_s // rows,)
            rix = lambda i: (off + i, 0)
            nb = rows // c
            s_shape = state.shape
            s_spec = pl.BlockSpec((nb, H_D, DK_D, DV_D), lambda i: (i, 0, 0, 0))
            sem = ("parallel",)
            scratch = []
        row = lambda w: pl.BlockSpec((rows, w), rix)
        args = [q, k, v, gate] + ([la, gnorm] if gla else []) + [c1, c2] + ([] if seq else [state])
        specs = [row(256), row(256), row(512), row(512)] + ([row(256), full(gnorm)] if gla else []) + [full(c1), full(c2)]
        specs += [] if seq else [s_spec]
        n_rows = n_prompt if seq else n_s
        o_rix = (lambda b, i: (b * per + i, 0)) if seq else (lambda i: (i, 0))
        o, s_out = pl.pallas_call(
            functools.partial(_lin_kernel, gla=gla, c=c, seq=seq, n_tiles=n_tiles, glast=glast),
            grid=grid,
            in_specs=specs,
            out_specs=[pl.BlockSpec((rows, 512), o_rix), s_spec],
            out_shape=[jax.ShapeDtypeStruct((n_rows, 512), BF16), jax.ShapeDtypeStruct(s_shape, F32)],
            scratch_shapes=scratch,
            compiler_params=_params(*sem),
            name=("gla" if gla else "ret") + ("_prompt" if seq else "_sample"),
        )(*args)
        res.append((o, s_out))
    o = jnp.concatenate([res[0][0], res[1][0]], 0)
    return o, res[0][1], res[1][1]


def _online_softmax_step(s, m_ref, l_ref, acc_ref, idx, v):
    m_old = m_ref[idx]
    m_new = jnp.maximum(m_old, jnp.max(s, axis=-1, keepdims=True))
    a = jnp.exp(m_old - m_new)
    p = jnp.exp(s - m_new)
    l_ref[idx] = a * l_ref[idx] + jnp.sum(p, axis=-1, keepdims=True)
    acc_ref[idx] = a * acc_ref[idx] + _dot(_bf(p), v)
    m_ref[idx] = m_new


def _causal_bias(tq, tk):
    r = lax.broadcasted_iota(jnp.int32, (tq, tk), 0)
    c = lax.broadcasted_iota(jnp.int32, (tq, tk), 1)
    return jnp.where(r >= c, 0.0, NEG_BIG).astype(F32)


def _mla_prompt_kernel(q_ref, kv_ref, wvu_ref, o_ref, m_ref, l_ref, acc_ref):
    i, j = pl.program_id(1), pl.program_id(2)
    tq = q_ref.shape[0]
    scale = (NOPE_B + ROPE_B) ** -0.5
    qw = KV_LORA + LANES

    @pl.when(j == 0)
    def _():
        m_ref[...] = jnp.full_like(m_ref, NEG_BIG)
        l_ref[...] = jnp.zeros_like(l_ref)
        acc_ref[...] = jnp.zeros_like(acc_ref)

    def step(masked):
        kv = kv_ref[...]
        bias = _causal_bias(tq, tq) if masked else None
        for h in range(H_B):
            s = _dot_nt(q_ref[:, h * qw:(h + 1) * qw], kv) * scale
            if masked:
                s = s + bias
            _online_softmax_step(s, m_ref, l_ref, acc_ref, h, kv[:, 0:KV_LORA])

    @pl.when(j < i)
    def _():
        step(False)

    @pl.when(j == i)
    def _():
        step(True)
        for h in range(H_B):
            lat = acc_ref[h] / l_ref[h]
            o_ref[:, h * V_B:(h + 1) * V_B] = _bf(_dot(_bf(lat), wvu_ref[h]))


def _mla_prompt(qm, kvm, w_vu, b_prompt, t_len, tq):
    n_q = t_len // tq
    qw = H_B * (KV_LORA + LANES)
    return pl.pallas_call(
        _mla_prompt_kernel,
        grid=(b_prompt, n_q, n_q),
        in_specs=[pl.BlockSpec((tq, qw), lambda b, i, j: (b * n_q + i, 0)),
                  pl.BlockSpec((tq, KV_LORA + LANES), lambda b, i, j: (b * n_q + jnp.minimum(i, j), 0)),
                  pl.BlockSpec(w_vu.shape, lambda b, i, j: (0, 0, 0))],
        out_specs=pl.BlockSpec((tq, H_B * V_B), lambda b, i, j: (b * n_q + i, 0)),
        out_shape=jax.ShapeDtypeStruct((b_prompt * t_len, H_B * V_B), BF16),
        scratch_shapes=[pltpu.VMEM((H_B, tq, 1), F32), pltpu.VMEM((H_B, tq, 1), F32),
                        pltpu.VMEM((H_B, tq, KV_LORA), F32)],
        compiler_params=_params("parallel", "parallel", "arbitrary"),
        name="mla_prompt",
    )(qm, kvm, w_vu)


def _lambda(lam_ref, lam_init):
    l = lam_ref[...]
    a = jnp.sum(l[0:1] * l[1:2], axis=-1, keepdims=True)
    b = jnp.sum(l[2:3] * l[3:4], axis=-1, keepdims=True)
    return jnp.exp(a) - jnp.exp(b) + lam_init


def _diff_prompt_kernel(q_ref, k_ref, v_ref, lam_ref, gsub_ref, o_ref, m_ref, l_ref, acc_ref, *, lam_init):
    i, j = pl.program_id(1), pl.program_id(2)
    tq = q_ref.shape[0]
    scale = D_C ** -0.5

    @pl.when(j == 0)
    def _():
        m_ref[...] = jnp.full_like(m_ref, NEG_BIG)
        l_ref[...] = jnp.zeros_like(l_ref)
        acc_ref[...] = jnp.zeros_like(acc_ref)

    def step(masked):
        bias = _causal_bias(tq, tq) if masked else None
        for g in range(KV_C):
            kg = k_ref[:, g * LANES:(g + 1) * LANES]
            vg = v_ref[:, g * LANES:(g + 1) * LANES]
            for rs in range(2 * REP_C):
                idx = g * 2 * REP_C + rs
                s = _dot_nt(q_ref[:, idx * LANES:(idx + 1) * LANES], kg) * scale
                if masked:
                    s = s + bias
                _online_softmax_step(s, m_ref, l_ref, acc_ref, idx, vg)

    @pl.when(j < i)
    def _():
        step(False)

    @pl.when(j == i)
    def _():
        step(True)
        lam = _lambda(lam_ref, lam_init)
        for gr in range(KV_C * REP_C):
            o = acc_ref[2 * gr] / l_ref[2 * gr] - lam * (acc_ref[2 * gr + 1] / l_ref[2 * gr + 1])
            o_ref[:, gr * LANES:(gr + 1) * LANES] = _bf(_rms(o) * gsub_ref[...] * (1.0 - lam_init))


def _diff_prompt(qdm, kcm, vcm, lam4, g_sub, lam_init, b_prompt, t_len, tq):
    n_q = t_len // tq
    n_maps = KV_C * REP_C * 2
    return pl.pallas_call(
        functools.partial(_diff_prompt_kernel, lam_init=lam_init),
        grid=(b_prompt, n_q, n_q),
        in_specs=[pl.BlockSpec((tq, n_maps * LANES), lambda b, i, j: (b * n_q + i, 0)),
                  pl.BlockSpec((tq, KV_C * LANES), lambda b, i, j: (b * n_q + jnp.minimum(i, j), 0)),
                  pl.BlockSpec((tq, KV_C * LANES), lambda b, i, j: (b * n_q + jnp.minimum(i, j), 0)),
                  pl.BlockSpec(lam4.shape, lambda b, i, j: (0, 0)),
                  pl.BlockSpec(g_sub.shape, lambda b, i, j: (0, 0))],
        out_specs=pl.BlockSpec((tq, H_C * 2 * D_C), lambda b, i, j: (b * n_q + i, 0)),
        out_shape=jax.ShapeDtypeStruct((b_prompt * t_len, H_C * 2 * D_C), BF16),
        scratch_shapes=[pltpu.VMEM((n_maps, tq, 1), F32), pltpu.VMEM((n_maps, tq, 1), F32),
                        pltpu.VMEM((n_maps, tq, LANES), F32)],
        compiler_params=_params("parallel", "parallel", "arbitrary"),
        name="diff_prompt",
    )(qdm, kcm, vcm, lam4, g_sub)


def _page_copies(pt_ref, caches, bufs, sems, layer, step, slot, pg, n_groups):
    b = step // n_groups
    p0 = (step % n_groups) * pg
    out = []
    for p in range(pg):
        page = pt_ref[b, p0 + p]
        for cache, buf, sem in zip(caches, bufs, sems):
            out.append(pltpu.make_async_copy(cache.at[layer, page], buf.at[slot, p], sem.at[slot]))
    return out


def _prefetch_pages(pt_ref, caches, bufs, sems, layer, pg, n_groups):
    step = pl.program_id(0) * n_groups + pl.program_id(1)
    n_steps = pl.num_programs(0) * n_groups
    slot = step % 2

    @pl.when(step == 0)
    def _():
        for cp in _page_copies(pt_ref, caches, bufs, sems, layer, step, slot, pg, n_groups):
            cp.start()

    @pl.when(step + 1 < n_steps)
    def _():
        for cp in _page_copies(pt_ref, caches, bufs, sems, layer, step + 1, 1 - slot, pg, n_groups):
            cp.start()

    for cp in _page_copies(pt_ref, caches, bufs, sems, layer, step, slot, pg, n_groups):
        cp.wait()
    return slot


def _mla_decode_kernel(pt_ref, q_ref, kvn_ref, wvu_ref, ckv_hbm, kr_hbm, o_ref,
                       ckv_buf, kr_buf, sem_c, sem_r, m_ref, l_ref, acc_ref, *, layer, pg, n_groups, pc):
    gi = pl.program_id(1)
    slot = _prefetch_pages(pt_ref, (ckv_hbm, kr_hbm), (ckv_buf, kr_buf), (sem_c, sem_r), layer, pg, n_groups)
    scale = (NOPE_B + ROPE_B) ** -0.5
    page = ckv_buf.shape[2]
    ts = q_ref.shape[0] // H_B

    @pl.when(gi == 0)
    def _():
        m_ref[...] = jnp.full_like(m_ref, NEG_BIG)
        l_ref[...] = jnp.zeros_like(l_ref)
        acc_ref[...] = jnp.zeros_like(acc_ref)

    q = q_ref[...]
    q_lat, q_rope = q[:, 0:KV_LORA], q[:, KV_LORA:KV_LORA + ROPE_B]
    for ch in range(pg // pc):
        kc = _bf(ckv_buf[slot, ch * pc:(ch + 1) * pc].reshape(pc * page, KV_LORA))
        kr = _bf(kr_buf[slot, ch * pc:(ch + 1) * pc].reshape(pc * page, ROPE_B))
        s = (_dot_nt(q_lat, kc) + _dot_nt(q_rope, kr)) * scale
        _online_softmax_step(s, m_ref, l_ref, acc_ref, 0, kc)

    @pl.when(gi == n_groups - 1)
    def _():
        kvn = kvn_ref[...]
        s = _dot_nt(q, kvn) * scale
        r = lax.broadcasted_iota(jnp.int32, s.shape, 0) % ts
        c = lax.broadcasted_iota(jnp.int32, s.shape, 1)
        s = jnp.where(r >= c, s, NEG_BIG)
        _online_softmax_step(s, m_ref, l_ref, acc_ref, 0, kvn[:, 0:KV_LORA])
        lat = acc_ref[0] / l_ref[0]
        for h in range(H_B):
            o_ref[:, h * V_B:(h + 1) * V_B] = _bf(_dot(_bf(lat[h * ts:(h + 1) * ts]), wvu_ref[h]))


def _mla_decode(page_table, q_s, kv_new, w_vu, cache_ckv, cache_kr, layer, pg, pc):
    bs, n_pages = page_table.shape
    rows = q_s.shape[1]
    ts = rows // H_B
    page = cache_ckv.shape[2]
    n_groups = n_pages // pg
    kern = functools.partial(_mla_decode_kernel, layer=layer, pg=pg, n_groups=n_groups, pc=pc)
    return pl.pallas_call(
        kern,
        grid_spec=pltpu.PrefetchScalarGridSpec(
            num_scalar_prefetch=1,
            grid=(bs, n_groups),
            in_specs=[pl.BlockSpec((None, rows, KV_LORA + LANES), lambda b, g, pt: (b, 0, 0)),
                      pl.BlockSpec((None, ts, KV_LORA + LANES), lambda b, g, pt: (b, 0, 0)),
                      pl.BlockSpec(w_vu.shape, lambda b, g, pt: (0, 0, 0)),
                      pl.BlockSpec(memory_space=pl.ANY), pl.BlockSpec(memory_space=pl.ANY)],
            out_specs=pl.BlockSpec((None, ts, H_B * V_B), lambda b, g, pt: (b, 0, 0)),
            scratch_shapes=[pltpu.VMEM((2, pg, page, KV_LORA), F32), pltpu.VMEM((2, pg, page, ROPE_B), F32),
                            pltpu.SemaphoreType.DMA((2,)), pltpu.SemaphoreType.DMA((2,)),
                            pltpu.VMEM((1, rows, 1), F32), pltpu.VMEM((1, rows, 1), F32),
                            pltpu.VMEM((1, rows, KV_LORA), F32)]),
        out_shape=jax.ShapeDtypeStruct((bs, ts, H_B * V_B), BF16),
        compiler_params=_params("arbitrary", "arbitrary"),
        name="mla_decode",
    )(page_table, q_s, kv_new, w_vu, cache_ckv, cache_kr)


def _diff_decode_kernel(pt_ref, q_ref, kn_ref, vn_ref, lam_ref, gsub_ref, k_hbm, v_hbm, o_ref,
                        k_buf, v_buf, sem_k, sem_v, m_ref, l_ref, acc_ref, *, layer, pg, n_groups, pc, lam_init):
    gi = pl.program_id(1)
    slot = _prefetch_pages(pt_ref, (k_hbm, v_hbm), (k_buf, v_buf), (sem_k, sem_v), layer, pg, n_groups)
    scale = D_C ** -0.5
    prow = k_buf.shape[2]
    rows = q_ref.shape[0]
    ts = rows // (KV_C * REP_C * 2)

    @pl.when(gi == 0)
    def _():
        m_ref[...] = jnp.full_like(m_ref, NEG_BIG)
        l_ref[...] = jnp.zeros_like(l_ref)
        acc_ref[...] = jnp.zeros_like(acc_ref)

    q = q_ref[...]

    def group_bias(n_cols):
        rg = lax.broadcasted_iota(jnp.int32, (rows, n_cols), 0) // (rows // KV_C)
        cg = lax.broadcasted_iota(jnp.int32, (rows, n_cols), 1) % KV_C
        return rg == cg

    same = group_bias(pc * prow)
    for ch in range(pg // pc):
        kk = _bf(k_buf[slot, ch * pc:(ch + 1) * pc].reshape(pc * prow, LANES))
        vv = _bf(v_buf[slot, ch * pc:(ch + 1) * pc].reshape(pc * prow, LANES))
        s = jnp.where(same, _dot_nt(q, kk) * scale, NEG_BIG)
        _online_softmax_step(s, m_ref, l_ref, acc_ref, 0, vv)

    @pl.when(gi == n_groups - 1)
    def _():
        kn = kn_ref[...]
        s = _dot_nt(q, kn) * scale
        r = lax.broadcasted_iota(jnp.int32, s.shape, 0) % ts
        c = lax.broadcasted_iota(jnp.int32, s.shape, 1) // KV_C
        s = jnp.where(group_bias(ts * KV_C) & (r >= c), s, NEG_BIG)
        _online_softmax_step(s, m_ref, l_ref, acc_ref, 0, vn_ref[...])
        o = acc_ref[0] / l_ref[0]
        lam = _lambda(lam_ref, lam_init)
        for gr in range(KV_C * REP_C):
            o1 = o[(2 * gr) * ts:(2 * gr + 1) * ts]
            o2 = o[(2 * gr + 1) * ts:(2 * gr + 2) * ts]
            o_ref[:, gr * LANES:(gr + 1) * LANES] = _bf(_rms(o1 - lam * o2) * gsub_ref[...] * (1.0 - lam_init))


def _diff_decode(page_table, q_s, k_new, v_new, lam4, g_sub, cache_k, cache_v, layer, lam_init, pg, pc):
    bs, n_pages = page_table.shape
    rows = q_s.shape[1]
    ts = rows // (KV_C * REP_C * 2)
    prow = cache_k.shape[2]
    n_groups = n_pages // pg
    kern = functools.partial(_diff_decode_kernel, layer=layer, pg=pg, n_groups=n_groups, pc=pc, lam_init=lam_init)
    return pl.pallas_call(
        kern,
        grid_spec=pltpu.PrefetchScalarGridSpec(
            num_scalar_prefetch=1,
            grid=(bs, n_groups),
            in_specs=[pl.BlockSpec((None, rows, LANES), lambda b, g, pt: (b, 0, 0)),
                      pl.BlockSpec((None, ts * KV_C, LANES), lambda b, g, pt: (b, 0, 0)),
                      pl.BlockSpec((None, ts * KV_C, LANES), lambda b, g, pt: (b, 0, 0)),
                      pl.BlockSpec(lam4.shape, lambda b, g, pt: (0, 0)),
                      pl.BlockSpec(g_sub.shape, lambda b, g, pt: (0, 0)),
                      pl.BlockSpec(memory_space=pl.ANY), pl.BlockSpec(memory_space=pl.ANY)],
            out_specs=pl.BlockSpec((None, ts, H_C * 2 * D_C), lambda b, g, pt: (b, 0, 0)),
            scratch_shapes=[pltpu.VMEM((2, pg, prow, LANES), F32), pltpu.VMEM((2, pg, prow, LANES), F32),
                            pltpu.SemaphoreType.DMA((2,)), pltpu.SemaphoreType.DMA((2,)),
                            pltpu.VMEM((1, rows, 1), F32), pltpu.VMEM((1, rows, 1), F32),
                            pltpu.VMEM((1, rows, LANES), F32)]),
        out_shape=jax.ShapeDtypeStruct((bs, ts, H_C * 2 * D_C), BF16),
        compiler_params=_params("arbitrary", "arbitrary"),
        name="diff_decode",
    )(page_table, q_s, k_new, v_new, lam4, g_sub, cache_k, cache_v)


def _layernorm(z, g, b):
    mu = jnp.mean(z, axis=-1, keepdims=True)
    d = z - mu
    var = jnp.mean(d * d, axis=-1, keepdims=True)
    return d * lax.rsqrt(var + EPS) * g + b


def _out_proj_kernel(o1_ref, o2_ref, w1_ref, w2_ref, x_ref, g_ref, b_ref, y_ref):
    y = _dot(o1_ref[...], w1_ref[...]) + _dot(o2_ref[...], w2_ref[...])
    y_ref[...] = _layernorm(ALPHA * x_ref[...] + y, g_ref[...], b_ref[...])


def _out_proj(o1, o2, w_out, x, g, b, tm):
    n = x.shape[0]
    half = o1.shape[1]
    row = lambda w: pl.BlockSpec((tm, w), lambda i: (i, 0))
    vec = pl.BlockSpec((1, D_MODEL), lambda i: (0, 0))
    return pl.pallas_call(
        _out_proj_kernel,
        grid=(n // tm,),
        in_specs=[row(half), row(half), pl.BlockSpec((half, D_MODEL), lambda i: (0, 0)),
                  pl.BlockSpec((half, D_MODEL), lambda i: (1, 0)), row(D_MODEL), vec, vec],
        out_specs=row(D_MODEL),
        out_shape=jax.ShapeDtypeStruct((n, D_MODEL), F32),
        compiler_params=_params("parallel"),
        name="out_proj_ln",
    )(o1, o2, w_out, w_out, x, g, b)


def _mlp_kernel(x_ref, w1_ref, w2_ref, g_ref, b_ref, y_ref, acc_ref):
    k = pl.program_id(1)

    @pl.when(k == 0)
    def _():
        acc_ref[...] = jnp.zeros_like(acc_ref)

    h = jnp.maximum(_dot(_bf(x_ref[...]), w1_ref[...]), 0.0)
    acc_ref[...] += _dot(_bf(h * h), w2_ref[...])

    @pl.when(k == pl.num_programs(1) - 1)
    def _():
        y_ref[...] = _layernorm(ALPHA * x_ref[...] + acc_ref[...], g_ref[...], b_ref[...])


def _mlp(x, w1, w2, g, b, tm, tf):
    n = x.shape[0]
    vec = pl.BlockSpec((1, D_MODEL), lambda i, k: (0, 0))
    return pl.pallas_call(
        _mlp_kernel,
        grid=(n // tm, D_FF // tf),
        in_specs=[pl.BlockSpec((tm, D_MODEL), lambda i, k: (i, 0)),
                  pl.BlockSpec((D_MODEL, tf), lambda i, k: (0, k)),
                  pl.BlockSpec((tf, D_MODEL), lambda i, k: (k, 0)), vec, vec],
        out_specs=pl.BlockSpec((tm, D_MODEL), lambda i, k: (i, 0)),
        out_shape=jax.ShapeDtypeStruct((n, D_MODEL), F32),
        scratch_shapes=[pltpu.VMEM((tm, D_MODEL), F32)],
        compiler_params=_params("parallel", "arbitrary"),
        name="mlp_ln",
    )(x, w1, w2, g, b)


def _pad_cols(w, to):
    return jnp.pad(w, ((0, 0), (0, to - w.shape[1])))


def _prep_even(w_in, w_qu, w_ku):
    w_in_p = _bf(_pad_cols(w_in, 2304))
    nope = w_qu[:, :, :NOPE_B].reshape(Q_LORA, H_B * NOPE_B)
    rope = jnp.pad(w_qu[:, :, NOPE_B:], ((0, 0), (0, 0), (0, LANES - ROPE_B))).reshape(Q_LORA, H_B * LANES)
    w_qu_p = _bf(jnp.concatenate([nope, rope], -1))
    w_ku_t = _bf(jnp.transpose(w_ku, (1, 2, 0)))
    return w_in_p, w_qu_p, w_ku_t


def _prep_odd(w_in, w_gu):
    return _bf(_pad_cols(w_in, 2688)), _bf(jnp.pad(w_gu, ((0, LANES - GK_RANK), (0, 0))))


def kernel(x_prompt, x_sample, state_ret, cache_mla_ckv, cache_mla_krope, cache_diff_k, cache_diff_v, state_gla, page_table, w_in_even, g_q_lora, g_kv_lora, w_q_up, w_k_up, w_v_up, w_out_even, w_in_odd, lam_q1, lam_k1, lam_q2, lam_k2, g_subln, w_gate_up, b_gate_up, g_gla_norm, w_out_odd, ln1_g, ln1_b, ln2_g, ln2_b, w_ff1, w_ff2):
    bp, tp, _ = x_prompt.shape
    bs, ts, _ = x_sample.shape
    n_p, n_s = bp * tp, bs * ts
    n_pages, page = page_table.shape[1], cache_mla_ckv.shape[2]
    past_len = n_pages * page
    assert LIN_TILE % ts == 0 and n_s % LIN_TILE == 0 and tp % LIN_TILE == 0

    tm = _tile(math.gcd(tp, n_s), 512)
    tq = _tile(tp, TQ_PREF)
    pg = _tile(n_pages, 32)
    pc = _tile(pg, 8)

    pos = jnp.concatenate([jnp.arange(tp, dtype=jnp.int32), past_len + jnp.tile(jnp.arange(ts, dtype=jnp.int32), bs)])
    tab_ret = _rope_table(pos, RET_THETA, DK_A, DK_A, 2)
    tab_mla = _rope_table(pos, ROPE_THETA, ROPE_B, ROPE_B, 1)
    tab_diff = _rope_table(pos, ROPE_THETA, ROT_C, D_C, 2)

    cache_k = cache_diff_k.reshape(cache_diff_k.shape[:2] + (page * KV_C, 2 * D_C))
    cache_v = cache_diff_v.reshape(cache_diff_v.shape[:2] + (page * KV_C, 2 * D_C))

    x = jnp.concatenate([x_prompt.reshape(n_p, D_MODEL), x_sample.reshape(n_s, D_MODEL)], 0)
    row2 = lambda v: v.reshape(1, -1)
    outs = {k: [] for k in ("p_ret", "p_ckv", "p_kr", "p_dk", "p_dv", "p_gla", "s_ret", "s_ckv", "s_kr", "s_dk", "s_dv", "s_gla")}
    for l in range(DEPTH):
        j = l // 2
        if l % 2 == 0:
            w_in_p, w_qu_p, w_ku_t = _prep_even(w_in_even[j], w_q_up[j], w_k_up[j])
            w_vu = _bf(jnp.transpose(w_v_up[j], (1, 0, 2)))
            qa, ka, va, ga, qm, ckv, kr, kvm = _proj_even(x, w_in_p, tab_ret, tab_mla, row2(g_q_lora[j]),
                                                          row2(g_kv_lora[j]), w_qu_p, w_ku_t, tp, n_p, tm)
            o_a, st_p, st_s = _lin_attn(qa, ka, va, ga, None, None, state_ret[j], gla=False, n_prompt=n_p, t_len=tp,
                                        b_prompt=bp, c_s=ts)
            o_bp = _mla_prompt(qm, kvm, w_vu, bp, tp, tq)
            qw = KV_LORA + LANES
            q_s = qm[n_p:].reshape(bs, ts, H_B, qw).transpose(0, 2, 1, 3).reshape(bs, H_B * ts, qw)
            o_bs = _mla_decode(page_table, q_s, kvm[n_p:].reshape(bs, ts, qw), w_vu, cache_mla_ckv, cache_mla_krope,
                               j, pg, pc)
            o_b = jnp.concatenate([o_bp, o_bs.reshape(n_s, H_B * V_B)], 0)
            x = _out_proj(o_a, o_b, _bf(w_out_even[j]), x, row2(ln1_g[l]), row2(ln1_b[l]), tm)
            outs["p_ret"].append(st_p)
            outs["s_ret"].append(st_s)
            outs["p_ckv"].append(ckv[:n_p].reshape(bp, tp, KV_LORA))
            outs["s_ckv"].append(ckv[n_p:].reshape(bs, ts, KV_LORA))
            outs["p_kr"].append(kr[:n_p].reshape(bp, tp, ROPE_B))
            outs["s_kr"].append(kr[n_p:].reshape(bs, ts, ROPE_B))
        else:
            lam_init = 0.8 - 0.6 * math.exp(-0.3 * l)
            w_in_p, w_gu_p = _prep_odd(w_in_odd[j], w_gate_up[j])
            lam4 = jnp.stack([lam_q1[j], lam_k1[j], lam_q2[j], lam_k2[j]])
            g_sub = row2(g_subln[j])
            qdm, kc, vc, kcm, vcm, qd, kd, vd, gd, la = _proj_odd(x, w_in_p, tab_diff, w_gu_p, row2(b_gate_up[j]),
                                                                  tp, n_p, tm)
            o_d, st_p, st_s = _lin_attn(qd, kd, vd, gd, la, row2(g_gla_norm[j]), state_gla[j], gla=True, n_prompt=n_p,
                                        t_len=tp, b_prompt=bp, c_s=ts)
            o_cp = _diff_prompt(qdm, kcm, vcm, lam4, g_sub, lam_init, bp, tp, tq)
            n_maps = KV_C * REP_C * 2
            q_s = qdm[n_p:].reshape(bs, ts, n_maps, LANES).transpose(0, 2, 1, 3).reshape(bs, n_maps * ts, LANES)
            k_new = kcm[n_p:].reshape(bs, ts * KV_C, LANES)
            v_new = vcm[n_p:].reshape(bs, ts * KV_C, LANES)
            o_cs = _diff_decode(page_table, q_s, k_new, v_new, lam4, g_sub, cache_k, cache_v, j, lam_init, pg, pc)
            o_c = jnp.concatenate([o_cp, o_cs.reshape(n_s, H_C * 2 * D_C)], 0)
            x = _out_proj(o_c, o_d, _bf(w_out_odd[j]), x, row2(ln1_g[l]), row2(ln1_b[l]), tm)
            outs["p_dk"].append(kc[:n_p].reshape(bp, tp, KV_C, 2 * D_C))
            outs["s_dk"].append(kc[n_p:].reshape(bs, ts, KV_C, 2 * D_C))
            outs["p_dv"].append(vc[:n_p].reshape(bp, tp, KV_C, 2 * D_C))
            outs["s_dv"].append(vc[n_p:].reshape(bs, ts, KV_C, 2 * D_C))
            outs["p_gla"].append(st_p)
            outs["s_gla"].append(st_s)
        x = _mlp(x, _bf(w_ff1[l]), _bf(w_ff2[l]), row2(ln2_g[l]), row2(ln2_b[l]), _tile(math.gcd(n_p, n_s), 1024),
                 _tile(D_FF, 1024))
    st = lambda k: jnp.stack(outs[k])
    return (x[:n_p].reshape(bp, tp, D_MODEL), x[n_p:].reshape(bs, ts, D_MODEL),
            st("p_ret"), st("p_ckv"), st("p_kr"), st("p_dk"), st("p_dv"), st("p_gla"),
            st("s_ret"), st("s_ckv"), st("s_kr"), st("s_dk"), st("s_dv"), st("s_gla"))
```

```python
import functools
import math

import numpy as np
import jax
import jax.numpy as jnp
from jax import lax
from jax.experimental import pallas as pl
from jax.experimental.pallas import tpu as pltpu

F32 = jnp.float32
BF16 = jnp.bfloat16

D_MODEL = 1024
DEPTH = 4
H_A, DK_A, DV_A = 4, 64, 128
RET_THETA = 10000.0
H_B, Q_LORA, KV_LORA, NOPE_B, ROPE_B, V_B = 4, 384, 256, 128, 64, 128
H_C, KV_C, D_C = 4, 2, 64
REP_C = H_C // KV_C
ROT_C = D_C // 4
H_D, DK_D, DV_D = 4, 64, 128
GK_RANK = 16
GATE_NORM = 16.0
D_FF = 4 * D_MODEL
ROPE_THETA = 500000.0
ALPHA = (2 * DEPTH) ** 0.25
EPS = 1e-5
EVEN_SPLITS = (H_A * DK_A, H_A * DK_A, H_A * DV_A, H_A * DV_A, Q_LORA, KV_LORA, ROPE_B)
ODD_SPLITS = (H_C * 2 * D_C, KV_C * 2 * D_C, KV_C * 2 * D_C, H_D * DK_D, H_D * DK_D, H_D * DV_D, H_D * DV_D, GK_RANK)

LANES = 128
LIN_TILE = 64
NEG_BIG = -1e30
TQ_PREF = 512
LOG2E = math.log2(math.e)
QSCALE_MLA = (NOPE_B + ROPE_B) ** -0.5 * LOG2E
QSCALE_DIFF = D_C ** -0.5 * LOG2E
VMEM_LIMIT = 52 * 1024 * 1024


def _dot(a, b):
    return jnp.dot(a, b, preferred_element_type=F32)


def _dot_nt(a, b):
    return lax.dot_general(a, b, (((1,), (1,)), ((), ())), preferred_element_type=F32)


def _dot_tn(a, b):
    return lax.dot_general(a, b, (((0,), (0,)), ((), ())), preferred_element_type=F32)


def _bf(x):
    return x.astype(BF16)


def _rms(x):
    return x * lax.rsqrt(jnp.mean(x * x, axis=-1, keepdims=True) + EPS)


def _silu(x):
    return x / (1.0 + jnp.exp(-x))


def _tile(n, pref):
    t = min(n, pref)
    while n % t:
        t -= 8
    return t


def _params(*sem):
    return pltpu.CompilerParams(dimension_semantics=sem, vmem_limit_bytes=VMEM_LIMIT)


def _rope_table(pos, theta, rot, group, n_groups):
    half = rot // 2
    inv = theta ** (-jnp.arange(half, dtype=F32) / half)
    ang = pos.astype(F32)[:, None] * inv[None, :]
    cos, sin = jnp.cos(ang), jnp.sin(ang)
    n = pos.shape[0]
    one = jnp.ones((n, group - rot), F32)
    zh = jnp.zeros((n, half), F32)
    zr = jnp.zeros((n, group - rot), F32)
    c = jnp.concatenate([cos, cos, one], -1)
    s1 = jnp.concatenate([-sin, zh, zr], -1)
    s2 = jnp.concatenate([zh, sin, zr], -1)
    pad = jnp.zeros((n, LANES - n_groups * group), F32)
    cat = lambda t: jnp.concatenate([t] * n_groups + [pad], -1)
    return jnp.concatenate([cat(c), cat(s1), cat(s2)], -1)


def _rope_slab(x, tab, half):
    c, s1, s2 = tab[:, 0:LANES], tab[:, LANES:2 * LANES], tab[:, 2 * LANES:3 * LANES]
    return x * c + pltpu.roll(x, LANES - half, 1) * s1 + pltpu.roll(x, half, 1) * s2


def _store_split(n_pt, p_ref, s_ref, val):
    @pl.when(pl.program_id(0) < n_pt)
    def _():
        p_ref[...] = val

    @pl.when(pl.program_id(0) >= n_pt)
    def _():
        s_ref[...] = val


def _proj_even_kernel(x_ref, w_ref, taba_ref, tabb_ref, gq_ref, gkv_ref, wqu_ref, wku_ref,
                      qa_ref, ka_ref, va_ref, ga_ref, qm_ref, kvm_ref, ckvp_ref, ckvs_ref, krp_ref, krs_ref, *, n_pt):
    h = _dot(_bf(x_ref[...]), w_ref[...])
    taba = taba_ref[...]
    tabb = tabb_ref[...]
    for j in range(2):
        sl = slice(j * LANES, (j + 1) * LANES)
        qa_ref[:, sl] = _rope_slab(h[:, j * LANES:(j + 1) * LANES], taba, DK_A // 2)
        ka_ref[:, sl] = _rope_slab(h[:, 256 + j * LANES:256 + (j + 1) * LANES], taba, DK_A // 2) * (DK_A ** -0.5)
    va_ref[...] = _bf(h[:, 512:1024])
    ga_ref[...] = h[:, 1024:1536]
    cqn = _rms(h[:, 1536:1920]) * gq_ref[...]
    ckvn = _rms(h[:, 1920:2176]) * gkv_ref[...]
    kr = _rope_slab(h[:, 2176:2304], tabb, ROPE_B // 2)
    _store_split(n_pt, ckvp_ref, ckvs_ref, ckvn)
    _store_split(n_pt, krp_ref, krs_ref, kr[:, 0:ROPE_B])
    kvm_ref[:, 0:KV_LORA] = _bf(ckvn)
    lane = lax.broadcasted_iota(jnp.int32, (1, LANES), 1)
    kvm_ref[:, KV_LORA:KV_LORA + LANES] = _bf(kr + (lane >= ROPE_B).astype(F32))
    q = _dot(_bf(cqn), wqu_ref[...])
    for hd in range(H_B):
        q_lat = _dot(_bf(q[:, hd * NOPE_B:(hd + 1) * NOPE_B]), wku_ref[hd])
        base = hd * (KV_LORA + LANES)
        qm_ref[:, base:base + KV_LORA] = _bf(q_lat * QSCALE_MLA)
        qr = _rope_slab(q[:, H_B * NOPE_B + hd * LANES:H_B * NOPE_B + (hd + 1) * LANES], tabb, ROPE_B // 2)
        qm_ref[:, base + KV_LORA:base + KV_LORA + LANES] = _bf(qr * QSCALE_MLA)


def _tab_index(tm, t_len, n_prompt):
    per = t_len // tm
    n_pt = n_prompt // tm
    return lambda i: (jnp.where(i < n_pt, i % per, per + i - n_pt), 0)


def _proj_even(x, w_in, taba, tabb, g_q, g_kv, w_qu, w_ku, t_len, n_prompt, tm):
    n = x.shape[0]
    row = lambda w: pl.BlockSpec((tm, w), lambda i: (i, 0))
    full = lambda a: pl.BlockSpec(a.shape, lambda i: (0,) * a.ndim)
    tix = _tab_index(tm, t_len, n_prompt)
    qw = H_B * (KV_LORA + LANES)
    n_pt = n_prompt // tm
    pix, six = _split_rows(tm, n_pt)
    outs = [(256, F32), (256, F32), (512, BF16), (512, F32), (qw, BF16), (KV_LORA + LANES, BF16)]
    split = [KV_LORA, ROPE_B]
    return pl.pallas_call(
        functools.partial(_proj_even_kernel, n_pt=n_pt),
        grid=(n // tm,),
        in_specs=[row(D_MODEL), full(w_in), pl.BlockSpec((tm, 3 * LANES), tix), pl.BlockSpec((tm, 3 * LANES), tix),
                  full(g_q), full(g_kv), full(w_qu), full(w_ku)],
        out_specs=[row(w) for w, _ in outs] + [pl.BlockSpec((tm, w), ix) for w in split for ix in (pix, six)],
        out_shape=[jax.ShapeDtypeStruct((n, w), d) for w, d in outs]
        + [jax.ShapeDtypeStruct((r, w), F32) for w in split for r in (n_prompt, n - n_prompt)],
        compiler_params=_params("arbitrary"),
        name="proj_even",
    )(x, w_in, taba, tabb, g_q, g_kv, w_qu, w_ku)


def _proj_odd_kernel(x_ref, w_ref, tabc_ref, wgu_ref, bgu_ref,
                     qdm_ref, kcm_ref, vcm_ref, qd_ref, kd_ref, vd_ref, gd_ref, la_ref,
                     kcp_ref, kcs_ref, vcp_ref, vcs_ref, *, n_pt):
    h = _dot(_bf(x_ref[...]), w_ref[...])
    tabc = tabc_ref[...]
    lane = lax.broadcasted_iota(jnp.int32, (1, LANES), 1)
    lo = (lane < D_C).astype(F32)
    hi = 1.0 - lo
    for j in range(4):
        qs = _rope_slab(h[:, j * LANES:(j + 1) * LANES], tabc, ROT_C // 2) * QSCALE_DIFF
        qdm_ref[:, (2 * j) * LANES:(2 * j + 1) * LANES] = _bf(qs * lo)
        qdm_ref[:, (2 * j + 1) * LANES:(2 * j + 2) * LANES] = _bf(qs * hi)
    kc = jnp.concatenate([_rope_slab(h[:, 512 + j * LANES:512 + (j + 1) * LANES], tabc, ROT_C // 2)
                          for j in range(KV_C)], axis=1)
    vc = h[:, 768:1024]
    _store_split(n_pt, kcp_ref, kcs_ref, kc)
    _store_split(n_pt, vcp_ref, vcs_ref, vc)
    kcm_ref[...] = _bf(kc)
    vcm_ref[...] = _bf(vc)
    qd_ref[...] = h[:, 1024:1280] * (DK_D ** -0.5)
    kd_ref[...] = h[:, 1280:1536]
    vd_ref[...] = _bf(h[:, 1536:2048])
    gd_ref[...] = h[:, 2048:2560]
    z = _dot(_bf(h[:, 2560:2688]), wgu_ref[...]) + bgu_ref[...]
    la_ref[...] = (jnp.minimum(z, 0.0) - jnp.log1p(jnp.exp(-jnp.abs(z)))) * (1.0 / GATE_NORM)


def _proj_odd(x, w_in, tabc, w_gu, b_gu, t_len, n_prompt, tm):
    n = x.shape[0]
    row = lambda w: pl.BlockSpec((tm, w), lambda i: (i, 0))
    full = lambda a: pl.BlockSpec(a.shape, lambda i: (0,) * a.ndim)
    tix = _tab_index(tm, t_len, n_prompt)
    n_pt = n_prompt // tm
    pix, six = _split_rows(tm, n_pt)
    outs = [(1024, BF16), (256, BF16), (256, BF16), (256, F32), (256, F32), (512, BF16), (512, F32), (256, F32)]
    split = [KV_C * 2 * D_C, KV_C * 2 * D_C]
    return pl.pallas_call(
        functools.partial(_proj_odd_kernel, n_pt=n_pt),
        grid=(n // tm,),
        in_specs=[row(D_MODEL), full(w_in), pl.BlockSpec((tm, 3 * LANES), tix), full(w_gu), full(b_gu)],
        out_specs=[row(w) for w, _ in outs] + [pl.BlockSpec((tm, w), ix) for w in split for ix in (pix, six)],
        out_shape=[jax.ShapeDtypeStruct((n, w), d) for w, d in outs]
        + [jax.ShapeDtypeStruct((r, w), F32) for w in split for r in (n_prompt, n - n_prompt)],
        compiler_params=_params("arbitrary"),
        name="proj_odd",
    )(x, w_in, tabc, w_gu, b_gu)


def _gla_consts(c):
    n = LIN_TILE
    i = np.arange(n)
    same = (i[:, None] // c) == (i[None, :] // c)
    mats = [same & (i[None, :] <= i[:, None]),
            same & (i[None, :] > i[:, None])]
    levels = []
    s = c // 2
    while s >= 1:
        levels.append(s)
        s //= 2
    masks = []
    for s in levels:
        blk, off = i // (2 * s), i % (2 * s)
        second = off >= s
        mid = blk * 2 * s + s
        mq = second[:, None] & (i[None, :] >= mid[:, None]) & (i[None, :] <= i[:, None])
        nk = (~second)[:, None] & (i[None, :] > i[:, None]) & (i[None, :] < mid[:, None])
        mats.append(mq | nk)
        masks.append(second[:, None] & (~second)[None, :] & (blk[:, None] == blk[None, :]))
    masks.append(i[:, None] == i[None, :])
    mall = np.concatenate(mats, 0).astype(np.float32)
    return jnp.asarray(mall, BF16), jnp.asarray(np.stack(masks).astype(np.float32)), len(levels)


def _ret_consts(c):
    n = LIN_TILE
    i = np.arange(n)
    log_g = np.log1p(-np.exp2(-5.0 - np.arange(H_A, dtype=np.float64)))
    same = (i[:, None] // c) == (i[None, :] // c)
    causal = same & (i[:, None] >= i[None, :])
    dmat = np.where(causal[None], np.exp(log_g[:, None, None] * (i[:, None] - i[None, :])[None]), 0.0)
    eq = np.exp(log_g[None, :] * ((i % c) + 1)[:, None])
    ek = np.exp(log_g[None, :] * (c - 1 - (i % c))[:, None])
    rep = lambda a: np.repeat(a, DK_A, axis=1)
    tabs = np.concatenate([rep(eq), rep(ek)], 1).astype(np.float32)
    glast = [float(np.exp(log_g[h] * c)) for h in range(H_A)]
    return jnp.asarray(dmat.astype(np.float32)), jnp.asarray(tabs), glast


def _split2(x):
    hi = _bf(x)
    return hi, _bf(x - hi.astype(F32))


def _lin_tile(q, k, v, gate, la, consts, state_get, state_put, *, gla, c, dk, dv, gnorm, glast):
    n_heads = q.shape[1] // dk
    n_sub = LIN_TILE // c
    if gla:
        mall, masks, nl = consts
        hi, lo = _split2(la)
        res = _dot(mall, jnp.concatenate([hi, lo], axis=1))
        res = res[:, 0:n_heads * dk] + res[:, n_heads * dk:]
        cum = res[0:LIN_TILE]
        qe = q * jnp.exp(cum)
        kk = k * jnp.exp(res[LIN_TILE:2 * LIN_TILE])
        el = [jnp.exp(res[(2 + l) * LIN_TILE:(3 + l) * LIN_TILE]) for l in range(nl)]
        qs = [q * e for e in el] + [q]
        ks = [k * e for e in el] + [k]
    else:
        dmat, tabs = consts
        qe = q * tabs[:, 0:n_heads * dk]
        kk = k * tabs[:, n_heads * dk:2 * n_heads * dk]
    v32 = v.astype(F32) if n_sub > 1 else None
    eye = None
    outs = []
    for h in range(n_heads):
        ksl = slice(h * dk, (h + 1) * dk)
        vsl = slice(h * dv, (h + 1) * dv)
        if gla:
            scores = masks[0] * _dot_nt(_bf(qs[0][:, ksl]), _bf(ks[0][:, ksl]))
            for l in range(1, nl + 1):
                scores = scores + masks[l] * _dot_nt(_bf(qs[l][:, ksl]), _bf(ks[l][:, ksl]))
        else:
            scores = dmat[h] * _dot_nt(_bf(q[:, ksl]), _bf(k[:, ksl]))
        o = _dot(_bf(scores), v[:, vsl])
        inter = []
        for u in range(n_sub):
            rs = slice(u * c, (u + 1) * c)
            s_old = state_get(u, h)
            inter.append(_dot(_bf(qe[rs, ksl]), _bf(s_old)))
            v_u = v[:, vsl] if n_sub == 1 else _bf(v32[rs, vsl])
            upd = _dot_tn(_bf(kk[rs, ksl]), v_u)
            if gla:
                if eye is None:
                    ii = lax.broadcasted_iota(jnp.int32, (dk, dk), 0)
                    jj = lax.broadcasted_iota(jnp.int32, (dk, dk), 1)
                    eye = (ii == jj).astype(F32)
                last = jnp.exp(cum[(u + 1) * c - 1:(u + 1) * c, ksl])
                dec = jnp.sum(eye * last, axis=1, keepdims=True)
            else:
                dec = glast[h]
            state_put(u, h, dec * s_old + upd)
        o = o + (inter[0] if n_sub == 1 else jnp.concatenate(inter, 0))
        o = _rms(o)
        if gnorm is not None:
            o = o * gnorm
        outs.append(o * _silu(gate[:, vsl]))
    return outs


def _lin_kernel(*refs, gla, c, seq, n_tiles, glast):
    it = iter(refs)
    q_ref, k_ref, v_ref, g_ref = next(it), next(it), next(it), next(it)
    la_ref = next(it) if gla else None
    gn_ref = next(it) if gla else None
    c1_ref, c2_ref = next(it), next(it)
    s_in_ref = None if seq else next(it)
    o_ref, s_out_ref = next(it), next(it)
    s_scr = next(it) if seq else None
    n_heads, dk, dv = H_D, DK_D, DV_D
    n_sub = LIN_TILE // c
    consts = (c1_ref[...], c2_ref[...], int(round(math.log2(c)))) if gla else (c1_ref[...], c2_ref[...])
    gnorm = gn_ref[...] if gla else None

    if seq:
        @pl.when(pl.program_id(1) == 0)
        def _():
            s_scr[...] = jnp.zeros_like(s_scr)

    for t in range(n_tiles):
        rows = slice(t * LIN_TILE, (t + 1) * LIN_TILE)
        if seq:
            get = lambda u, h: s_scr[h]

            def put(u, h, s):
                s_scr[h] = s
        else:
            get = lambda u, h, t=t: s_in_ref[t * n_sub + u, h]

            def put(u, h, s, t=t):
                s_out_ref[t * n_sub + u, h] = s
        outs = _lin_tile(q_ref[rows, :], k_ref[rows, :], v_ref[rows, :], g_ref[rows, :],
                         la_ref[rows, :] if gla else None, consts, get, put,
                         gla=gla, c=c, dk=dk, dv=dv, gnorm=gnorm, glast=glast)
        for h in range(n_heads):
            o_ref[rows, h * dv:(h + 1) * dv] = _bf(outs[h])

    if seq:
        @pl.when(pl.program_id(1) == pl.num_programs(1) - 1)
        def _():
            s_out_ref[0] = s_scr[...]


def _lin_attn(q, k, v, gate, la, gnorm, state, layer, *, gla, n_prompt, t_len, b_prompt, c_s):
    n = q.shape[0]
    n_s = n - n_prompt
    res = []
    for seq in (True, False):
        c = LIN_TILE if seq else c_s
        if gla:
            c1, c2, _ = _gla_consts(c)
            glast = None
        else:
            c1, c2, glast = _ret_consts(c)
        rows = _tile(t_len if seq else n_s, 256)
        n_tiles = rows // LIN_TILE
        full = lambda a: pl.BlockSpec(a.shape, lambda *_: (0,) * a.ndim)
        if seq:
            per = t_len // rows
            grid = (b_prompt, per)
            rix = lambda b, i: (b * per + i, 0)
            s_shape = (b_prompt, H_D, DK_D, DV_D)
            s_spec = pl.BlockSpec((1, H_D, DK_D, DV_D), lambda b, i: (b, 0, 0, 0))
            sem = ("parallel", "arbitrary")
            scratch = [pltpu.VMEM((H_D, DK_D, DV_D), F32)]
        else:
            off = n_prompt // rows
            grid = (n_s // rows,)
            rix = lambda i: (off + i, 0)
            nb = rows // c
            s_shape = state.shape[1:]
            s_spec = pl.BlockSpec((nb, H_D, DK_D, DV_D), lambda i: (i, 0, 0, 0))
            s_in_spec = pl.BlockSpec((None, nb, H_D, DK_D, DV_D), lambda i: (layer, i, 0, 0, 0))
            sem = ("parallel",)
            scratch = []
        row = lambda w: pl.BlockSpec((rows, w), rix)
        args = [q, k, v, gate] + ([la, gnorm] if gla else []) + [c1, c2] + ([] if seq else [state])
        specs = [row(256), row(256), row(512), row(512)] + ([row(256), full(gnorm)] if gla else []) + [full(c1), full(c2)]
        specs += [] if seq else [s_in_spec]
        n_rows = n_prompt if seq else n_s
        o_rix = (lambda b, i: (b * per + i, 0)) if seq else (lambda i: (i, 0))
        o, s_out = pl.pallas_call(
            functools.partial(_lin_kernel, gla=gla, c=c, seq=seq, n_tiles=n_tiles, glast=glast),
            grid=grid,
            in_specs=specs,
            out_specs=[pl.BlockSpec((rows, 512), o_rix), s_spec],
            out_shape=[jax.ShapeDtypeStruct((n_rows, 512), BF16), jax.ShapeDtypeStruct(s_shape, F32)],
            scratch_shapes=scratch,
            compiler_params=_params(*sem),
            name=("gla" if gla else "ret") + ("_prompt" if seq else "_sample"),
        )(*args)
        res.append((o, s_out))
    return res[0][0], res[1][0], res[0][1], res[1][1]


def _col_blocks(s):
    return [s[:, c * LANES:(c + 1) * LANES] for c in range(s.shape[1] // LANES)]


def _block_max(s):
    blocks = _col_blocks(s)
    mc = blocks[0]
    for b in blocks[1:]:
        mc = jnp.maximum(mc, b)
    return jnp.max(mc, axis=-1, keepdims=True)


def _flash_update(s, m_ref, acc_ref, idx, v1):
    m_old = m_ref[idx]
    m_new = jnp.maximum(m_old, _block_max(s))
    alpha = jnp.exp2(m_old - m_new)
    p = jnp.concatenate([_bf(jnp.exp2(b - m_new)) for b in _col_blocks(s)], axis=1)
    reps = acc_ref.shape[-1] // LANES
    acc_ref[idx] = jnp.concatenate([alpha] * reps, axis=1) * acc_ref[idx] + _dot(p, v1)
    m_ref[idx] = m_new


def _causal_bias(tq, tk):
    r = lax.broadcasted_iota(jnp.int32, (tq, tk), 0)
    c = lax.broadcasted_iota(jnp.int32, (tq, tk), 1)
    return jnp.where(r >= c, 0.0, NEG_BIG).astype(F32)


def _mla_prompt_kernel(q_ref, kv_ref, wvu_ref, o_ref, m_ref, acc_ref):
    i, j = pl.program_id(1), pl.program_id(2)
    tq = q_ref.shape[0]
    qw = KV_LORA + LANES

    @pl.when(j == 0)
    def _():
        m_ref[...] = jnp.full_like(m_ref, NEG_BIG)
        acc_ref[...] = jnp.zeros_like(acc_ref)

    def step(masked):
        kv = kv_ref[...]
        bias = _causal_bias(tq, tq) if masked else None
        for h in range(H_B):
            s = _dot_nt(q_ref[:, h * qw:(h + 1) * qw], kv)
            if masked:
                s = s + bias
            _flash_update(s, m_ref, acc_ref, h, kv)

    @pl.when(j < i)
    def _():
        step(False)

    @pl.when(j == i)
    def _():
        step(True)
        for h in range(H_B):
            acc = acc_ref[h]
            lat = acc[:, 0:KV_LORA] / acc[:, qw - 1:qw]
            o_ref[:, h * V_B:(h + 1) * V_B] = _bf(_dot(_bf(lat), wvu_ref[h]))


def _mla_prompt(qm, kvm, w_vu, b_prompt, t_len, tq):
    n_q = t_len // tq
    qw = H_B * (KV_LORA + LANES)
    return pl.pallas_call(
        _mla_prompt_kernel,
        grid=(b_prompt, n_q, n_q),
        in_specs=[pl.BlockSpec((tq, qw), lambda b, i, j: (b * n_q + i, 0)),
                  pl.BlockSpec((tq, KV_LORA + LANES), lambda b, i, j: (b * n_q + jnp.minimum(i, j), 0)),
                  pl.BlockSpec(w_vu.shape, lambda b, i, j: (0, 0, 0))],
        out_specs=pl.BlockSpec((tq, H_B * V_B), lambda b, i, j: (b * n_q + i, 0)),
        out_shape=jax.ShapeDtypeStruct((b_prompt * t_len, H_B * V_B), BF16),
        scratch_shapes=[pltpu.VMEM((H_B, tq, LANES), F32), pltpu.VMEM((H_B, tq, KV_LORA + LANES), F32)],
        compiler_params=_params("parallel", "parallel", "arbitrary"),
        name="mla_prompt",
    )(qm, kvm, w_vu)


def _lambda(lam_ref, lam_init):
    l = lam_ref[...]
    a = jnp.sum(l[0:1] * l[1:2], axis=-1, keepdims=True)
    b = jnp.sum(l[2:3] * l[3:4], axis=-1, keepdims=True)
    return jnp.exp(a) - jnp.exp(b) + lam_init


def _diff_prompt_kernel(q_ref, k_ref, v_ref, lam_ref, gsub_ref, o_ref, m_ref, acc_ref, *, lam_init):
    i, j = pl.program_id(1), pl.program_id(2)
    tq = q_ref.shape[0]

    @pl.when(j == 0)
    def _():
        m_ref[...] = jnp.full_like(m_ref, NEG_BIG)
        acc_ref[...] = jnp.zeros_like(acc_ref)

    def step(masked):
        bias = _causal_bias(tq, tq) if masked else None
        ones = jnp.ones((tq, LANES), BF16)
        for g in range(KV_C):
            kg = k_ref[:, g * LANES:(g + 1) * LANES]
            v1 = jnp.concatenate([v_ref[:, g * LANES:(g + 1) * LANES], ones], axis=1)
            for rs in range(2 * REP_C):
                idx = g * 2 * REP_C + rs
                s = _dot_nt(q_ref[:, idx * LANES:(idx + 1) * LANES], kg)
                if masked:
                    s = s + bias
                _flash_update(s, m_ref, acc_ref, idx, v1)

    @pl.when(j < i)
    def _():
        step(False)

    @pl.when(j == i)
    def _():
        step(True)
        lam = _lambda(lam_ref, lam_init)
        for gr in range(KV_C * REP_C):
            a1, a2 = acc_ref[2 * gr], acc_ref[2 * gr + 1]
            o = a1[:, 0:LANES] / a1[:, LANES:2 * LANES] - lam * (a2[:, 0:LANES] / a2[:, LANES:2 * LANES])
            o_ref[:, gr * LANES:(gr + 1) * LANES] = _bf(_rms(o) * gsub_ref[...] * (1.0 - lam_init))


def _diff_prompt(qdm, kcm, vcm, lam4, g_sub, lam_init, b_prompt, t_len, tq):
    n_q = t_len // tq
    n_maps = KV_C * REP_C * 2
    return pl.pallas_call(
        functools.partial(_diff_prompt_kernel, lam_init=lam_init),
        grid=(b_prompt, n_q, n_q),
        in_specs=[pl.BlockSpec((tq, n_maps * LANES), lambda b, i, j: (b * n_q + i, 0)),
                  pl.BlockSpec((tq, KV_C * LANES), lambda b, i, j: (b * n_q + jnp.minimum(i, j), 0)),
                  pl.BlockSpec((tq, KV_C * LANES), lambda b, i, j: (b * n_q + jnp.minimum(i, j), 0)),
                  pl.BlockSpec(lam4.shape, lambda b, i, j: (0, 0)),
                  pl.BlockSpec(g_sub.shape, lambda b, i, j: (0, 0))],
        out_specs=pl.BlockSpec((tq, H_C * 2 * D_C), lambda b, i, j: (b * n_q + i, 0)),
        out_shape=jax.ShapeDtypeStruct((b_prompt * t_len, H_C * 2 * D_C), BF16),
        scratch_shapes=[pltpu.VMEM((n_maps, tq, LANES), F32), pltpu.VMEM((n_maps, tq, 2 * LANES), F32)],
        compiler_params=_params("parallel", "parallel", "arbitrary"),
        name="diff_prompt",
    )(qdm, kcm, vcm, lam4, g_sub)


def _page_copies(pt_ref, caches, bufs, sems, layer, step, slot, pg, n_groups):
    b = step // n_groups
    p0 = (step % n_groups) * pg
    out = []
    for p in range(pg):
        page = pt_ref[b, p0 + p]
        for cache, buf, sem in zip(caches, bufs, sems):
            out.append(pltpu.make_async_copy(cache.at[layer, page], buf.at[slot, p], sem.at[slot]))
    return out


def _prefetch_pages(pt_ref, caches, bufs, sems, layer, pg, n_groups):
    step = pl.program_id(0) * n_groups + pl.program_id(1)
    n_steps = pl.num_programs(0) * n_groups
    slot = step % 2

    @pl.when(step == 0)
    def _():
        for cp in _page_copies(pt_ref, caches, bufs, sems, layer, step, slot, pg, n_groups):
            cp.start()

    @pl.when(step + 1 < n_steps)
    def _():
        for cp in _page_copies(pt_ref, caches, bufs, sems, layer, step + 1, 1 - slot, pg, n_groups):
            cp.start()

    for cp in _page_copies(pt_ref, caches, bufs, sems, layer, step, slot, pg, n_groups):
        cp.wait()
    return slot


def _decode_update(s_parts, v_parts, tail, m_ref, l_ref, acc_ref):
    m_old = m_ref[...]
    mx = _block_max(s_parts[0])
    for sp in s_parts[1:]:
        mx = jnp.maximum(mx, _block_max(sp))
    if tail is not None:
        mx = jnp.maximum(mx, jnp.max(tail[0], axis=-1, keepdims=True))
    m_new = jnp.maximum(m_old, mx)
    alpha = jnp.exp2(m_old - m_new)
    lsum, pv = None, None
    for sp, vp in zip(s_parts, v_parts):
        blocks = [jnp.exp2(b - m_new) for b in _col_blocks(sp)]
        for b in blocks:
            lsum = b if lsum is None else lsum + b
        d = _dot(jnp.concatenate([_bf(b) for b in blocks], axis=1), vp)
        pv = d if pv is None else pv + d
    lrow = jnp.sum(lsum, axis=-1, keepdims=True)
    if tail is not None:
        pt = jnp.exp2(tail[0] - m_new[:, 0:1])
        lrow = lrow + jnp.sum(pt, axis=-1, keepdims=True)
        pv = pv + _dot(_bf(pt), tail[1])
    l_ref[...] = alpha * l_ref[...] + lrow
    acc_ref[...] = jnp.concatenate([alpha] * (acc_ref.shape[-1] // LANES), axis=1) * acc_ref[...] + pv
    m_ref[...] = m_new


def _mla_decode_kernel(pt_ref, q_ref, kvn_ref, wvu_ref, ckv_hbm, krt_hbm, o_ref,
                       ckv_buf, krt_buf, sem_c, sem_r, m_ref, l_ref, acc_ref, *, layer, pg, n_groups, pc):
    gi = pl.program_id(1)
    slot = _prefetch_pages(pt_ref, (ckv_hbm, krt_hbm), (ckv_buf, krt_buf), (sem_c, sem_r), layer, pg, n_groups)
    page = ckv_buf.shape[2]
    ts = q_ref.shape[0] // H_B

    @pl.when(gi == 0)
    def _():
        m_ref[...] = jnp.full_like(m_ref, NEG_BIG)
        l_ref[...] = jnp.zeros_like(l_ref)
        acc_ref[...] = jnp.zeros_like(acc_ref)

    def step(last):
        q = q_ref[...]
        q_lat, q_rope = q[:, 0:KV_LORA], q[:, KV_LORA:KV_LORA + ROPE_B]
        s_parts, v_parts = [], []
        for ch in range(pg // pc):
            kc = _bf(ckv_buf[slot, ch * pc:(ch + 1) * pc].reshape(pc * page, KV_LORA))
            krt = jnp.concatenate([_bf(krt_buf[slot, p]) for p in range(ch * pc, (ch + 1) * pc)], axis=1)
            s_parts.append(_dot_nt(q_lat, kc) + _dot(q_rope, krt))
            v_parts.append(kc)
        tail = None
        if last:
            kvn = kvn_ref[...]
            s = _dot_nt(q[:, 0:KV_LORA + ROPE_B], kvn[:, 0:KV_LORA + ROPE_B])
            r = lax.broadcasted_iota(jnp.int32, s.shape, 0) % ts
            c = lax.broadcasted_iota(jnp.int32, s.shape, 1)
            tail = (jnp.where(r >= c, s, NEG_BIG), kvn[:, 0:KV_LORA])
        _decode_update(s_parts, v_parts, tail, m_ref, l_ref, acc_ref)

    if n_groups > 1:
        @pl.when(gi < n_groups - 1)
        def _():
            step(False)

    @pl.when(gi == n_groups - 1)
    def _():
        step(True)
        lat = acc_ref[...] / l_ref[:, 0:1]
        for h in range(H_B):
            o_ref[:, h * V_B:(h + 1) * V_B] = _bf(_dot(_bf(lat[h * ts:(h + 1) * ts]), wvu_ref[h]))


def _mla_decode(page_table, q_s, kv_new, w_vu, cache_ckv, cache_kr, layer, pg, pc):
    bs, n_pages = page_table.shape
    rows = q_s.shape[1]
    ts = rows // H_B
    page = cache_ckv.shape[2]
    n_groups = n_pages // pg
    kern = functools.partial(_mla_decode_kernel, layer=layer, pg=pg, n_groups=n_groups, pc=pc)
    return pl.pallas_call(
        kern,
        grid_spec=pltpu.PrefetchScalarGridSpec(
            num_scalar_prefetch=1,
            grid=(bs, n_groups),
            in_specs=[pl.BlockSpec((None, rows, KV_LORA + LANES), lambda b, g, pt: (b, 0, 0)),
                      pl.BlockSpec((None, ts, KV_LORA + LANES), lambda b, g, pt: (b, 0, 0)),
                      pl.BlockSpec(w_vu.shape, lambda b, g, pt: (0, 0, 0)),
                      pl.BlockSpec(memory_space=pl.ANY), pl.BlockSpec(memory_space=pl.ANY)],
            out_specs=pl.BlockSpec((None, ts, H_B * V_B), lambda b, g, pt: (b, 0, 0)),
            scratch_shapes=[pltpu.VMEM((2, pg, page, KV_LORA), F32), pltpu.VMEM((2, pg, ROPE_B, page), F32),
                            pltpu.SemaphoreType.DMA((2,)), pltpu.SemaphoreType.DMA((2,)),
                            pltpu.VMEM((rows, LANES), F32), pltpu.VMEM((rows, LANES), F32),
                            pltpu.VMEM((rows, KV_LORA), F32)]),
        out_shape=jax.ShapeDtypeStruct((bs, ts, H_B * V_B), BF16),
        compiler_params=_params("arbitrary", "arbitrary"),
        name="mla_decode",
    )(page_table, q_s, kv_new, w_vu, cache_ckv, cache_kr)


def _diff_decode_kernel(pt_ref, q_ref, kn_ref, vn_ref, lam_ref, gsub_ref, k_hbm, v_hbm, o_ref,
                        k_buf, v_buf, sem_k, sem_v, m_ref, l_ref, acc_ref, *, layer, pg, n_groups, pc, lam_init):
    gi = pl.program_id(1)
    slot = _prefetch_pages(pt_ref, (k_hbm, v_hbm), (k_buf, v_buf), (sem_k, sem_v), layer, pg, n_groups)
    prow = k_buf.shape[2]
    rows = q_ref.shape[0]
    ts = rows // (KV_C * REP_C * 2)

    @pl.when(gi == 0)
    def _():
        m_ref[...] = jnp.full_like(m_ref, NEG_BIG)
        l_ref[...] = jnp.zeros_like(l_ref)
        acc_ref[...] = jnp.zeros_like(acc_ref)

    def group_match(n_cols):
        rg = lax.broadcasted_iota(jnp.int32, (rows, n_cols), 0) // (rows // KV_C)
        cg = lax.broadcasted_iota(jnp.int32, (rows, n_cols), 1) % KV_C
        return rg == cg

    def step(last):
        q = q_ref[...]
        same = group_match(pc * prow)
        s_parts, v_parts = [], []
        for ch in range(pg // pc):
            kk = _bf(k_buf[slot, ch * pc:(ch + 1) * pc].reshape(pc * prow, LANES))
            s_parts.append(jnp.where(same, _dot_nt(q, kk), NEG_BIG))
            v_parts.append(_bf(v_buf[slot, ch * pc:(ch + 1) * pc].reshape(pc * prow, LANES)))
        tail = None
        if last:
            s = _dot_nt(q, kn_ref[...])
            r = lax.broadcasted_iota(jnp.int32, s.shape, 0) % ts
            c = lax.broadcasted_iota(jnp.int32, s.shape, 1) // KV_C
            tail = (jnp.where(group_match(ts * KV_C) & (r >= c), s, NEG_BIG), vn_ref[...])
        _decode_update(s_parts, v_parts, tail, m_ref, l_ref, acc_ref)

    if n_groups > 1:
        @pl.when(gi < n_groups - 1)
        def _():
            step(False)

    @pl.when(gi == n_groups - 1)
    def _():
        step(True)
        o = acc_ref[...] / l_ref[:, 0:1]
        lam = _lambda(lam_ref, lam_init)
        for gr in range(KV_C * REP_C):
            o1 = o[(2 * gr) * ts:(2 * gr + 1) * ts]
            o2 = o[(2 * gr + 1) * ts:(2 * gr + 2) * ts]
            o_ref[:, gr * LANES:(gr + 1) * LANES] = _bf(_rms(o1 - lam * o2) * gsub_ref[...] * (1.0 - lam_init))


def _diff_decode(page_table, q_s, k_new, v_new, lam4, g_sub, cache_k, cache_v, layer, lam_init, pg, pc):
    bs, n_pages = page_table.shape
    rows = q_s.shape[1]
    ts = rows // (KV_C * REP_C * 2)
    prow = cache_k.shape[2]
    n_groups = n_pages // pg
    kern = functools.partial(_diff_decode_kernel, layer=layer, pg=pg, n_groups=n_groups, pc=pc, lam_init=lam_init)
    return pl.pallas_call(
        kern,
        grid_spec=pltpu.PrefetchScalarGridSpec(
            num_scalar_prefetch=1,
            grid=(bs, n_groups),
            in_specs=[pl.BlockSpec((None, rows, LANES), lambda b, g, pt: (b, 0, 0)),
                      pl.BlockSpec((None, ts * KV_C, LANES), lambda b, g, pt: (b, 0, 0)),
                      pl.BlockSpec((None, ts * KV_C, LANES), lambda b, g, pt: (b, 0, 0)),
                      pl.BlockSpec(lam4.shape, lambda b, g, pt: (0, 0)),
                      pl.BlockSpec(g_sub.shape, lambda b, g, pt: (0, 0)),
                      pl.BlockSpec(memory_space=pl.ANY), pl.BlockSpec(memory_space=pl.ANY)],
            out_specs=pl.BlockSpec((None, ts, H_C * 2 * D_C), lambda b, g, pt: (b, 0, 0)),
            scratch_shapes=[pltpu.VMEM((2, pg, prow, LANES), F32), pltpu.VMEM((2, pg, prow, LANES), F32),
                            pltpu.SemaphoreType.DMA((2,)), pltpu.SemaphoreType.DMA((2,)),
                            pltpu.VMEM((rows, LANES), F32), pltpu.VMEM((rows, LANES), F32),
                            pltpu.VMEM((rows, LANES), F32)]),
        out_shape=jax.ShapeDtypeStruct((bs, ts, H_C * 2 * D_C), BF16),
        compiler_params=_params("arbitrary", "arbitrary"),
        name="diff_decode",
    )(page_table, q_s, k_new, v_new, lam4, g_sub, cache_k, cache_v)


def _layernorm(z, g, b):
    mu = jnp.mean(z, axis=-1, keepdims=True)
    d = z - mu
    var = jnp.mean(d * d, axis=-1, keepdims=True)
    return d * lax.rsqrt(var + EPS) * g + b


def _split_rows(tm, n_pt):
    return (lambda i, *_: (jnp.minimum(i, n_pt - 1), 0)), (lambda i, *_: (jnp.maximum(i - n_pt, 0), 0))


def _out_proj_kernel(o1p_ref, o1s_ref, o2p_ref, o2s_ref, w1_ref, w2_ref, x_ref, g_ref, b_ref, y_ref, *, n_pt):
    def body(o1_ref, o2_ref):
        y = _dot(o1_ref[...], w1_ref[...]) + _dot(o2_ref[...], w2_ref[...])
        y_ref[...] = _layernorm(ALPHA * x_ref[...] + y, g_ref[...], b_ref[...])

    @pl.when(pl.program_id(0) < n_pt)
    def _():
        body(o1p_ref, o2p_ref)

    @pl.when(pl.program_id(0) >= n_pt)
    def _():
        body(o1s_ref, o2s_ref)


def _out_proj(o1p, o1s, o2p, o2s, w_out, layer, x, g, b, tm):
    n = x.shape[0]
    half = o1p.shape[1]
    n_pt = o1p.shape[0] // tm
    pix, six = _split_rows(tm, n_pt)
    row = lambda w: pl.BlockSpec((tm, w), lambda i: (i, 0))
    vec = pl.BlockSpec((1, D_MODEL), lambda i: (0, 0))
    return pl.pallas_call(
        functools.partial(_out_proj_kernel, n_pt=n_pt),
        grid=(n // tm,),
        in_specs=[pl.BlockSpec((tm, half), pix), pl.BlockSpec((tm, half), six),
                  pl.BlockSpec((tm, half), pix), pl.BlockSpec((tm, half), six),
                  pl.BlockSpec((None, half, D_MODEL), lambda i: (layer, 0, 0)),
                  pl.BlockSpec((None, half, D_MODEL), lambda i: (layer, 1, 0)), row(D_MODEL), vec, vec],
        out_specs=row(D_MODEL),
        out_shape=jax.ShapeDtypeStruct((n, D_MODEL), F32),
        compiler_params=_params("arbitrary"),
        name="out_proj_ln",
    )(o1p, o1s, o2p, o2s, w_out, w_out, x, g, b)


def _mlp_kernel(x_ref, w1_ref, w2_ref, g_ref, b_ref, *rest, n_pt):
    acc_ref = rest[-1]
    i, k = pl.program_id(0), pl.program_id(1)
    last = k == pl.num_programs(1) - 1

    @pl.when(k == 0)
    def _():
        acc_ref[...] = jnp.zeros_like(acc_ref)

    h = jnp.maximum(_dot(_bf(x_ref[...]), w1_ref[...]), 0.0)
    acc_ref[...] += _dot(_bf(h * h), w2_ref[...])

    def finish(y_ref):
        y_ref[...] = _layernorm(ALPHA * x_ref[...] + acc_ref[...], g_ref[...], b_ref[...])

    if n_pt is None:
        pl.when(last)(lambda: finish(rest[0]))
    else:
        pl.when(last & (i < n_pt))(lambda: finish(rest[0]))
        pl.when(last & (i >= n_pt))(lambda: finish(rest[1]))


def _mlp(x, w1, w2, layer, g, b, tm, tf, n_prompt=None):
    n = x.shape[0]
    vec = pl.BlockSpec((1, D_MODEL), lambda i, k: (0, 0))
    if n_prompt is None:
        n_pt = None
        out_specs = pl.BlockSpec((tm, D_MODEL), lambda i, k: (i, 0))
        out_shape = jax.ShapeDtypeStruct((n, D_MODEL), F32)
    else:
        n_pt = n_prompt // tm
        pix, six = _split_rows(tm, n_pt)
        out_specs = [pl.BlockSpec((tm, D_MODEL), pix), pl.BlockSpec((tm, D_MODEL), six)]
        out_shape = [jax.ShapeDtypeStruct((n_prompt, D_MODEL), F32), jax.ShapeDtypeStruct((n - n_prompt, D_MODEL), F32)]
    return pl.pallas_call(
        functools.partial(_mlp_kernel, n_pt=n_pt),
        grid=(n // tm, D_FF // tf),
        in_specs=[pl.BlockSpec((tm, D_MODEL), lambda i, k: (i, 0)),
                  pl.BlockSpec((None, D_MODEL, tf), lambda i, k: (layer, 0, k)),
                  pl.BlockSpec((None, tf, D_MODEL), lambda i, k: (layer, k, 0)), vec, vec],
        out_specs=out_specs,
        out_shape=out_shape,
        scratch_shapes=[pltpu.VMEM((tm, D_MODEL), F32)],
        compiler_params=_params("arbitrary", "arbitrary"),
        name="mlp_ln",
    )(x, w1, w2, g, b)


def _pad_cols(w, to):
    return jnp.pad(w, ((0, 0), (0, to - w.shape[1])))


def _prep_even(w_in, w_qu, w_ku):
    w_in_p = _bf(_pad_cols(w_in, 2304))
    nope = w_qu[:, :, :NOPE_B].reshape(Q_LORA, H_B * NOPE_B)
    rope = jnp.pad(w_qu[:, :, NOPE_B:], ((0, 0), (0, 0), (0, LANES - ROPE_B))).reshape(Q_LORA, H_B * LANES)
    w_qu_p = _bf(jnp.concatenate([nope, rope], -1))
    w_ku_t = _bf(jnp.transpose(w_ku, (1, 2, 0)))
    return w_in_p, w_qu_p, w_ku_t


def _prep_odd(w_in, w_gu):
    return _bf(_pad_cols(w_in, 2688)), _bf(jnp.pad(w_gu, ((0, LANES - GK_RANK), (0, 0))))


def kernel(x_prompt, x_sample, state_ret, cache_mla_ckv, cache_mla_krope, cache_diff_k, cache_diff_v, state_gla, page_table, w_in_even, g_q_lora, g_kv_lora, w_q_up, w_k_up, w_v_up, w_out_even, w_in_odd, lam_q1, lam_k1, lam_q2, lam_k2, g_subln, w_gate_up, b_gate_up, g_gla_norm, w_out_odd, ln1_g, ln1_b, ln2_g, ln2_b, w_ff1, w_ff2):
    bp, tp, _ = x_prompt.shape
    bs, ts, _ = x_sample.shape
    n_p, n_s = bp * tp, bs * ts
    n_pages, page = page_table.shape[1], cache_mla_ckv.shape[2]
    past_len = n_pages * page
    assert LIN_TILE % ts == 0 and n_s % LIN_TILE == 0 and tp % LIN_TILE == 0

    tm = _tile(math.gcd(tp, n_s), 512)
    tq = _tile(tp, TQ_PREF)
    pg = _tile(n_pages, 32)
    pg_mla = _tile(n_pages, 64)
    pc = _tile(pg, 8)

    pos = jnp.concatenate([jnp.arange(tp, dtype=jnp.int32), past_len + jnp.tile(jnp.arange(ts, dtype=jnp.int32), bs)])
    tab_ret = _rope_table(pos, RET_THETA, DK_A, DK_A, 2)
    tab_mla = _rope_table(pos, ROPE_THETA, ROPE_B, ROPE_B, 1)
    tab_diff = _rope_table(pos, ROPE_THETA, ROT_C, D_C, 2)

    cache_krt = jnp.swapaxes(cache_mla_krope, 2, 3)
    cache_k = cache_diff_k.reshape(cache_diff_k.shape[:2] + (page * KV_C, 2 * D_C))
    cache_v = cache_diff_v.reshape(cache_diff_v.shape[:2] + (page * KV_C, 2 * D_C))

    x = jnp.concatenate([x_prompt.reshape(n_p, D_MODEL), x_sample.reshape(n_s, D_MODEL)], 0)
    row2 = lambda v: v.reshape(1, -1)
    w_ff1_b, w_ff2_b, w_out_even_b, w_out_odd_b = _bf(w_ff1), _bf(w_ff2), _bf(w_out_even), _bf(w_out_odd)
    outs = {k: [] for k in ("p_ret", "p_ckv", "p_kr", "p_dk", "p_dv", "p_gla", "s_ret", "s_ckv", "s_kr", "s_dk", "s_dv", "s_gla")}
    for l in range(DEPTH):
        j = l // 2
        if l % 2 == 0:
            w_in_p, w_qu_p, w_ku_t = _prep_even(w_in_even[j], w_q_up[j], w_k_up[j])
            w_vu = _bf(jnp.transpose(w_v_up[j], (1, 0, 2)))
            qa, ka, va, ga, qm, kvm, ckv_p, ckv_s, kr_p, kr_s = _proj_even(
                x, w_in_p, tab_ret, tab_mla, row2(g_q_lora[j]), row2(g_kv_lora[j]), w_qu_p, w_ku_t, tp, n_p, tm)
            o_ap, o_as, st_p, st_s = _lin_attn(qa, ka, va, ga, None, None, state_ret, j, gla=False, n_prompt=n_p,
                                               t_len=tp, b_prompt=bp, c_s=ts)
            o_bp = _mla_prompt(qm, kvm, w_vu, bp, tp, tq)
            qw = KV_LORA + LANES
            q_s = qm[n_p:].reshape(bs, ts, H_B, qw).transpose(0, 2, 1, 3).reshape(bs, H_B * ts, qw)
            o_bs = _mla_decode(page_table, q_s, kvm[n_p:].reshape(bs, ts, qw), w_vu, cache_mla_ckv, cache_krt,
                               j, pg_mla, pc)
            x = _out_proj(o_ap, o_as, o_bp, o_bs.reshape(n_s, H_B * V_B), w_out_even_b, j, x, row2(ln1_g[l]),
                          row2(ln1_b[l]), tm)
            outs["p_ret"].append(st_p)
            outs["s_ret"].append(st_s)
            outs["p_ckv"].append(ckv_p.reshape(bp, tp, KV_LORA))
            outs["s_ckv"].append(ckv_s.reshape(bs, ts, KV_LORA))
            outs["p_kr"].append(kr_p.reshape(bp, tp, ROPE_B))
            outs["s_kr"].append(kr_s.reshape(bs, ts, ROPE_B))
        else:
            lam_init = 0.8 - 0.6 * math.exp(-0.3 * l)
            w_in_p, w_gu_p = _prep_odd(w_in_odd[j], w_gate_up[j])
            lam4 = jnp.stack([lam_q1[j], lam_k1[j], lam_q2[j], lam_k2[j]])
            g_sub = row2(g_subln[j])
            qdm, kcm, vcm, qd, kd, vd, gd, la, kc_p, kc_s, vc_p, vc_s = _proj_odd(
                x, w_in_p, tab_diff, w_gu_p, row2(b_gate_up[j]), tp, n_p, tm)
            o_dp, o_ds, st_p, st_s = _lin_attn(qd, kd, vd, gd, la, row2(g_gla_norm[j]), state_gla, j, gla=True,
                                               n_prompt=n_p, t_len=tp, b_prompt=bp, c_s=ts)
            o_cp = _diff_prompt(qdm, kcm, vcm, lam4, g_sub, lam_init, bp, tp, tq)
            n_maps = KV_C * REP_C * 2
            q_s = qdm[n_p:].reshape(bs, ts, n_maps, LANES).transpose(0, 2, 1, 3).reshape(bs, n_maps * ts, LANES)
            k_new = kcm[n_p:].reshape(bs, ts * KV_C, LANES)
            v_new = vcm[n_p:].reshape(bs, ts * KV_C, LANES)
            o_cs = _diff_decode(page_table, q_s, k_new, v_new, lam4, g_sub, cache_k, cache_v, j, lam_init, pg, pc)
            x = _out_proj(o_cp, o_cs.reshape(n_s, H_C * 2 * D_C), o_dp, o_ds, w_out_odd_b, j, x, row2(ln1_g[l]),
                          row2(ln1_b[l]), tm)
            outs["p_dk"].append(kc_p.reshape(bp, tp, KV_C, 2 * D_C))
            outs["s_dk"].append(kc_s.reshape(bs, ts, KV_C, 2 * D_C))
            outs["p_dv"].append(vc_p.reshape(bp, tp, KV_C, 2 * D_C))
            outs["s_dv"].append(vc_s.reshape(bs, ts, KV_C, 2 * D_C))
            outs["p_gla"].append(st_p)
            outs["s_gla"].append(st_s)
        x = _mlp(x, w_ff1_b, w_ff2_b, l, row2(ln2_g[l]), row2(ln2_b[l]), _tile(math.gcd(n_p, n_s), 1024),
                 _tile(D_FF, 1024), n_prompt=n_p if l == DEPTH - 1 else None)
    st = lambda k: jnp.stack(outs[k])
    return (x[0].reshape(bp, tp, D_MODEL), x[1].reshape(bs, ts, D_MODEL),
            st("p_ret"), st("p_ckv"), st("p_kr"), st("p_dk"), st("p_dv"), st("p_gla"),
            st("s_ret"), st("s_ckv"), st("s_kr"), st("s_dk"), st("s_dv"), st("s_gla"))
```

```python
import functools
import math

import numpy as np
import jax
import jax.numpy as jnp
from jax import lax
from jax.experimental import pallas as pl
from jax.experimental.pallas import tpu as pltpu

F32 = jnp.float32
BF16 = jnp.bfloat16

D_MODEL = 1024
DEPTH = 4
H_A, DK_A, DV_A = 4, 64, 128
RET_THETA = 10000.0
H_B, Q_LORA, KV_LORA, NOPE_B, ROPE_B, V_B = 4, 384, 256, 128, 64, 128
H_C, KV_C, D_C = 4, 2, 64
REP_C = H_C // KV_C
ROT_C = D_C // 4
H_D, DK_D, DV_D = 4, 64, 128
GK_RANK = 16
GATE_NORM = 16.0
D_FF = 4 * D_MODEL
ROPE_THETA = 500000.0
ALPHA = (2 * DEPTH) ** 0.25
EPS = 1e-5
EVEN_SPLITS = (H_A * DK_A, H_A * DK_A, H_A * DV_A, H_A * DV_A, Q_LORA, KV_LORA, ROPE_B)
ODD_SPLITS = (H_C * 2 * D_C, KV_C * 2 * D_C, KV_C * 2 * D_C, H_D * DK_D, H_D * DK_D, H_D * DV_D, H_D * DV_D, GK_RANK)

LANES = 128
LIN_TILE = 64
NEG_BIG = -1e30
TQ_PREF = 512
LOG2E = math.log2(math.e)
QSCALE_MLA = (NOPE_B + ROPE_B) ** -0.5 * LOG2E
QSCALE_DIFF = D_C ** -0.5 * LOG2E
VMEM_LIMIT = 52 * 1024 * 1024


def _dot(a, b):
    return jnp.dot(a, b, preferred_element_type=F32)


def _dot_nt(a, b):
    return lax.dot_general(a, b, (((1,), (1,)), ((), ())), preferred_element_type=F32)


def _dot_tn(a, b):
    return lax.dot_general(a, b, (((0,), (0,)), ((), ())), preferred_element_type=F32)


def _bf(x):
    return x.astype(BF16)


def _rms(x):
    return x * lax.rsqrt(jnp.mean(x * x, axis=-1, keepdims=True) + EPS)


def _silu(x):
    return x / (1.0 + jnp.exp(-x))


def _tile(n, pref):
    t = min(n, pref)
    while n % t:
        t -= 8
    return t


def _params(*sem):
    return pltpu.CompilerParams(dimension_semantics=sem, vmem_limit_bytes=VMEM_LIMIT)


def _rope_table(pos, theta, rot, group, n_groups):
    half = rot // 2
    inv = theta ** (-jnp.arange(half, dtype=F32) / half)
    ang = pos.astype(F32)[:, None] * inv[None, :]
    cos, sin = jnp.cos(ang), jnp.sin(ang)
    n = pos.shape[0]
    one = jnp.ones((n, group - rot), F32)
    zh = jnp.zeros((n, half), F32)
    zr = jnp.zeros((n, group - rot), F32)
    c = jnp.concatenate([cos, cos, one], -1)
    s1 = jnp.concatenate([-sin, zh, zr], -1)
    s2 = jnp.concatenate([zh, sin, zr], -1)
    pad = jnp.zeros((n, LANES - n_groups * group), F32)
    cat = lambda t: jnp.concatenate([t] * n_groups + [pad], -1)
    return jnp.concatenate([cat(c), cat(s1), cat(s2)], -1)


def _rope_slab(x, tab, half):
    c, s1, s2 = tab[:, 0:LANES], tab[:, LANES:2 * LANES], tab[:, 2 * LANES:3 * LANES]
    return x * c + pltpu.roll(x, LANES - half, 1) * s1 + pltpu.roll(x, half, 1) * s2


def _store_split(n_pt, p_ref, s_ref, val):
    @pl.when(pl.program_id(0) < n_pt)
    def _():
        p_ref[...] = val

    @pl.when(pl.program_id(0) >= n_pt)
    def _():
        s_ref[...] = val


def _proj_even_kernel(x_ref, w_ref, taba_ref, tabb_ref, gq_ref, gkv_ref, wqu_ref, wku_ref,
                      qa_ref, ka_ref, va_ref, ga_ref, qm_ref, kvm_ref, ckvp_ref, ckvs_ref, krp_ref, krs_ref, *, n_pt):
    h = _dot(_bf(x_ref[...]), w_ref[...])
    taba = taba_ref[...]
    tabb = tabb_ref[...]
    for j in range(2):
        sl = slice(j * LANES, (j + 1) * LANES)
        qa_ref[:, sl] = _rope_slab(h[:, j * LANES:(j + 1) * LANES], taba, DK_A // 2)
        ka_ref[:, sl] = _rope_slab(h[:, 256 + j * LANES:256 + (j + 1) * LANES], taba, DK_A // 2) * (DK_A ** -0.5)
    va_ref[...] = _bf(h[:, 512:1024])
    ga_ref[...] = h[:, 1024:1536]
    cqn = _rms(h[:, 1536:1920]) * gq_ref[...]
    ckvn = _rms(h[:, 1920:2176]) * gkv_ref[...]
    kr = _rope_slab(h[:, 2176:2304], tabb, ROPE_B // 2)
    _store_split(n_pt, ckvp_ref, ckvs_ref, ckvn)
    _store_split(n_pt, krp_ref, krs_ref, kr[:, 0:ROPE_B])
    kvm_ref[:, 0:KV_LORA] = _bf(ckvn)
    kvm_ref[:, KV_LORA:KV_LORA + LANES] = _bf(kr)
    q = _dot(_bf(cqn), wqu_ref[...])
    for hd in range(H_B):
        q_lat = _dot(_bf(q[:, hd * NOPE_B:(hd + 1) * NOPE_B]), wku_ref[hd])
        base = hd * (KV_LORA + LANES)
        qm_ref[:, base:base + KV_LORA] = _bf(q_lat * QSCALE_MLA)
        qr = _rope_slab(q[:, H_B * NOPE_B + hd * LANES:H_B * NOPE_B + (hd + 1) * LANES], tabb, ROPE_B // 2)
        qm_ref[:, base + KV_LORA:base + KV_LORA + LANES] = _bf(qr * QSCALE_MLA)


def _tab_index(tm, t_len, n_prompt):
    per = t_len // tm
    n_pt = n_prompt // tm
    return lambda i: (jnp.where(i < n_pt, i % per, per + i - n_pt), 0)


def _proj_even(x, w_in, taba, tabb, g_q, g_kv, w_qu, w_ku, t_len, n_prompt, tm):
    n = x.shape[0]
    row = lambda w: pl.BlockSpec((tm, w), lambda i: (i, 0))
    full = lambda a: pl.BlockSpec(a.shape, lambda i: (0,) * a.ndim)
    tix = _tab_index(tm, t_len, n_prompt)
    qw = H_B * (KV_LORA + LANES)
    n_pt = n_prompt // tm
    pix, six = _split_rows(tm, n_pt)
    outs = [(256, F32), (256, F32), (512, BF16), (512, F32), (qw, BF16), (KV_LORA + LANES, BF16)]
    split = [KV_LORA, ROPE_B]
    return pl.pallas_call(
        functools.partial(_proj_even_kernel, n_pt=n_pt),
        grid=(n // tm,),
        in_specs=[row(D_MODEL), full(w_in), pl.BlockSpec((tm, 3 * LANES), tix), pl.BlockSpec((tm, 3 * LANES), tix),
                  full(g_q), full(g_kv), full(w_qu), full(w_ku)],
        out_specs=[row(w) for w, _ in outs] + [pl.BlockSpec((tm, w), ix) for w in split for ix in (pix, six)],
        out_shape=[jax.ShapeDtypeStruct((n, w), d) for w, d in outs]
        + [jax.ShapeDtypeStruct((r, w), F32) for w in split for r in (n_prompt, n - n_prompt)],
        compiler_params=_params("arbitrary"),
        name="proj_even",
    )(x, w_in, taba, tabb, g_q, g_kv, w_qu, w_ku)


def _proj_odd_kernel(x_ref, w_ref, tabc_ref, wgu_ref, bgu_ref,
                     qdm_ref, kcm_ref, vcm_ref, qd_ref, kd_ref, vd_ref, gd_ref, la_ref,
                     kcp_ref, kcs_ref, vcp_ref, vcs_ref, *, n_pt):
    h = _dot(_bf(x_ref[...]), w_ref[...])
    tabc = tabc_ref[...]
    lane = lax.broadcasted_iota(jnp.int32, (1, LANES), 1)
    lo = (lane < D_C).astype(F32)
    hi = 1.0 - lo
    for j in range(4):
        qs = _rope_slab(h[:, j * LANES:(j + 1) * LANES], tabc, ROT_C // 2) * QSCALE_DIFF
        qdm_ref[:, (2 * j) * LANES:(2 * j + 1) * LANES] = _bf(qs * lo)
        qdm_ref[:, (2 * j + 1) * LANES:(2 * j + 2) * LANES] = _bf(qs * hi)
    kc = jnp.concatenate([_rope_slab(h[:, 512 + j * LANES:512 + (j + 1) * LANES], tabc, ROT_C // 2)
                          for j in range(KV_C)], axis=1)
    vc = h[:, 768:1024]
    _store_split(n_pt, kcp_ref, kcs_ref, kc)
    _store_split(n_pt, vcp_ref, vcs_ref, vc)
    kcm_ref[...] = _bf(kc)
    vcm_ref[...] = _bf(vc)
    qd_ref[...] = h[:, 1024:1280] * (DK_D ** -0.5)
    kd_ref[...] = h[:, 1280:1536]
    vd_ref[...] = _bf(h[:, 1536:2048])
    gd_ref[...] = h[:, 2048:2560]
    z = _dot(_bf(h[:, 2560:2688]), wgu_ref[...]) + bgu_ref[...]
    la_ref[...] = (jnp.minimum(z, 0.0) - jnp.log1p(jnp.exp(-jnp.abs(z)))) * (1.0 / GATE_NORM)


def _proj_odd(x, w_in, tabc, w_gu, b_gu, t_len, n_prompt, tm):
    n = x.shape[0]
    row = lambda w: pl.BlockSpec((tm, w), lambda i: (i, 0))
    full = lambda a: pl.BlockSpec(a.shape, lambda i: (0,) * a.ndim)
    tix = _tab_index(tm, t_len, n_prompt)
    n_pt = n_prompt // tm
    pix, six = _split_rows(tm, n_pt)
    outs = [(1024, BF16), (256, BF16), (256, BF16), (256, F32), (256, F32), (512, BF16), (512, F32), (256, F32)]
    split = [KV_C * 2 * D_C, KV_C * 2 * D_C]
    return pl.pallas_call(
        functools.partial(_proj_odd_kernel, n_pt=n_pt),
        grid=(n // tm,),
        in_specs=[row(D_MODEL), full(w_in), pl.BlockSpec((tm, 3 * LANES), tix), full(w_gu), full(b_gu)],
        out_specs=[row(w) for w, _ in outs] + [pl.BlockSpec((tm, w), ix) for w in split for ix in (pix, six)],
        out_shape=[jax.ShapeDtypeStruct((n, w), d) for w, d in outs]
        + [jax.ShapeDtypeStruct((r, w), F32) for w in split for r in (n_prompt, n - n_prompt)],
        compiler_params=_params("arbitrary"),
        name="proj_odd",
    )(x, w_in, tabc, w_gu, b_gu)


def _gla_consts(c):
    n = LIN_TILE
    i = np.arange(n)
    same = (i[:, None] // c) == (i[None, :] // c)
    mats = [same & (i[None, :] <= i[:, None]),
            same & (i[None, :] > i[:, None])]
    levels = []
    s = c // 2
    while s >= 1:
        levels.append(s)
        s //= 2
    masks = []
    for s in levels:
        blk, off = i // (2 * s), i % (2 * s)
        second = off >= s
        mid = blk * 2 * s + s
        mq = second[:, None] & (i[None, :] >= mid[:, None]) & (i[None, :] <= i[:, None])
        nk = (~second)[:, None] & (i[None, :] > i[:, None]) & (i[None, :] < mid[:, None])
        mats.append(mq | nk)
        masks.append(second[:, None] & (~second)[None, :] & (blk[:, None] == blk[None, :]))
    masks.append(i[:, None] == i[None, :])
    mall = np.concatenate(mats, 0).astype(np.float32)
    return jnp.asarray(mall, BF16), jnp.asarray(np.stack(masks).astype(np.float32)), len(levels)


def _ret_consts(c):
    n = LIN_TILE
    i = np.arange(n)
    log_g = np.log1p(-np.exp2(-5.0 - np.arange(H_A, dtype=np.float64)))
    same = (i[:, None] // c) == (i[None, :] // c)
    causal = same & (i[:, None] >= i[None, :])
    dmat = np.where(causal[None], np.exp(log_g[:, None, None] * (i[:, None] - i[None, :])[None]), 0.0)
    eq = np.exp(log_g[None, :] * ((i % c) + 1)[:, None])
    ek = np.exp(log_g[None, :] * (c - 1 - (i % c))[:, None])
    rep = lambda a: np.repeat(a, DK_A, axis=1)
    tabs = np.concatenate([rep(eq), rep(ek)], 1).astype(np.float32)
    glast = [float(np.exp(log_g[h] * c)) for h in range(H_A)]
    return jnp.asarray(dmat.astype(np.float32)), jnp.asarray(tabs), glast


def _split2(x):
    hi = _bf(x)
    return hi, _bf(x - hi.astype(F32))


def _lin_tile(q, k, v, gate, la, consts, state_get, state_put, *, gla, c, dk, dv, gnorm, glast):
    n_heads = q.shape[1] // dk
    n_sub = LIN_TILE // c
    if gla:
        mall, masks, nl = consts
        hi, lo = _split2(la)
        res = _dot(mall, jnp.concatenate([hi, lo], axis=1))
        res = res[:, 0:n_heads * dk] + res[:, n_heads * dk:]
        cum = res[0:LIN_TILE]
        qe = q * jnp.exp(cum)
        kk = k * jnp.exp(res[LIN_TILE:2 * LIN_TILE])
        el = [jnp.exp(res[(2 + l) * LIN_TILE:(3 + l) * LIN_TILE]) for l in range(nl)]
        qs = [q * e for e in el] + [q]
        ks = [k * e for e in el] + [k]
    else:
        dmat, tabs = consts
        qe = q * tabs[:, 0:n_heads * dk]
        kk = k * tabs[:, n_heads * dk:2 * n_heads * dk]
    v32 = v.astype(F32) if n_sub > 1 else None
    eye = None
    outs = []
    for h in range(n_heads):
        ksl = slice(h * dk, (h + 1) * dk)
        vsl = slice(h * dv, (h + 1) * dv)
        if gla:
            scores = masks[0] * _dot_nt(_bf(qs[0][:, ksl]), _bf(ks[0][:, ksl]))
            for l in range(1, nl + 1):
                scores = scores + masks[l] * _dot_nt(_bf(qs[l][:, ksl]), _bf(ks[l][:, ksl]))
        else:
            scores = dmat[h] * _dot_nt(_bf(q[:, ksl]), _bf(k[:, ksl]))
        o = _dot(_bf(scores), v[:, vsl])
        inter = []
        for u in range(n_sub):
            rs = slice(u * c, (u + 1) * c)
            s_old = state_get(u, h)
            inter.append(_dot(_bf(qe[rs, ksl]), _bf(s_old)))
            v_u = v[:, vsl] if n_sub == 1 else _bf(v32[rs, vsl])
            upd = _dot_tn(_bf(kk[rs, ksl]), v_u)
            if gla:
                if eye is None:
                    ii = lax.broadcasted_iota(jnp.int32, (dk, dk), 0)
                    jj = lax.broadcasted_iota(jnp.int32, (dk, dk), 1)
                    eye = (ii == jj).astype(F32)
                last = jnp.exp(cum[(u + 1) * c - 1:(u + 1) * c, ksl])
                dec = jnp.sum(eye * last, axis=1, keepdims=True)
            else:
                dec = glast[h]
            state_put(u, h, dec * s_old + upd)
        o = o + (inter[0] if n_sub == 1 else jnp.concatenate(inter, 0))
        o = _rms(o)
        if gnorm is not None:
            o = o * gnorm
        outs.append(o * _silu(gate[:, vsl]))
    return outs


def _lin_kernel(*refs, gla, c, seq, n_tiles, glast):
    it = iter(refs)
    q_ref, k_ref, v_ref, g_ref = next(it), next(it), next(it), next(it)
    la_ref = next(it) if gla else None
    gn_ref = next(it) if gla else None
    c1_ref, c2_ref = next(it), next(it)
    s_in_ref = None if seq else next(it)
    o_ref, s_out_ref = next(it), next(it)
    s_scr = next(it) if seq else None
    n_heads, dk, dv = H_D, DK_D, DV_D
    n_sub = LIN_TILE // c
    consts = (c1_ref[...], c2_ref[...], int(round(math.log2(c)))) if gla else (c1_ref[...], c2_ref[...])
    gnorm = gn_ref[...] if gla else None

    if seq:
        @pl.when(pl.program_id(1) == 0)
        def _():
            s_scr[...] = jnp.zeros_like(s_scr)

    for t in range(n_tiles):
        rows = slice(t * LIN_TILE, (t + 1) * LIN_TILE)
        if seq:
            get = lambda u, h: s_scr[h]

            def put(u, h, s):
                s_scr[h] = s
        else:
            get = lambda u, h, t=t: s_in_ref[t * n_sub + u, h]

            def put(u, h, s, t=t):
                s_out_ref[t * n_sub + u, h] = s
        outs = _lin_tile(q_ref[rows, :], k_ref[rows, :], v_ref[rows, :], g_ref[rows, :],
                         la_ref[rows, :] if gla else None, consts, get, put,
                         gla=gla, c=c, dk=dk, dv=dv, gnorm=gnorm, glast=glast)
        for h in range(n_heads):
            o_ref[rows, h * dv:(h + 1) * dv] = _bf(outs[h])

    if seq:
        @pl.when(pl.program_id(1) == pl.num_programs(1) - 1)
        def _():
            s_out_ref[0] = s_scr[...]


def _lin_attn(q, k, v, gate, la, gnorm, state, layer, *, gla, n_prompt, t_len, b_prompt, c_s):
    n = q.shape[0]
    n_s = n - n_prompt
    res = []
    for seq in (True, False):
        c = LIN_TILE if seq else c_s
        if gla:
            c1, c2, _ = _gla_consts(c)
            glast = None
        else:
            c1, c2, glast = _ret_consts(c)
        rows = _tile(t_len if seq else n_s, 256)
        n_tiles = rows // LIN_TILE
        full = lambda a: pl.BlockSpec(a.shape, lambda *_: (0,) * a.ndim)
        if seq:
            per = t_len // rows
            grid = (b_prompt, per)
            rix = lambda b, i: (b * per + i, 0)
            s_shape = (b_prompt, H_D, DK_D, DV_D)
            s_spec = pl.BlockSpec((1, H_D, DK_D, DV_D), lambda b, i: (b, 0, 0, 0))
            sem = ("parallel", "arbitrary")
            scratch = [pltpu.VMEM((H_D, DK_D, DV_D), F32)]
        else:
            off = n_prompt // rows
            grid = (n_s // rows,)
            rix = lambda i: (off + i, 0)
            nb = rows // c
            s_shape = state.shape[1:]
            s_spec = pl.BlockSpec((nb, H_D, DK_D, DV_D), lambda i: (i, 0, 0, 0))
            s_in_spec = pl.BlockSpec((None, nb, H_D, DK_D, DV_D), lambda i: (layer, i, 0, 0, 0))
            sem = ("parallel",)
            scratch = []
        row = lambda w: pl.BlockSpec((rows, w), rix)
        args = [q, k, v, gate] + ([la, gnorm] if gla else []) + [c1, c2] + ([] if seq else [state])
        specs = [row(256), row(256), row(512), row(512)] + ([row(256), full(gnorm)] if gla else []) + [full(c1), full(c2)]
        specs += [] if seq else [s_in_spec]
        n_rows = n_prompt if seq else n_s
        o_rix = (lambda b, i: (b * per + i, 0)) if seq else (lambda i: (i, 0))
        o, s_out = pl.pallas_call(
            functools.partial(_lin_kernel, gla=gla, c=c, seq=seq, n_tiles=n_tiles, glast=glast),
            grid=grid,
            in_specs=specs,
            out_specs=[pl.BlockSpec((rows, 512), o_rix), s_spec],
            out_shape=[jax.ShapeDtypeStruct((n_rows, 512), BF16), jax.ShapeDtypeStruct(s_shape, F32)],
            scratch_shapes=scratch,
            compiler_params=_params(*sem),
            name=("gla" if gla else "ret") + ("_prompt" if seq else "_sample"),
        )(*args)
        res.append((o, s_out))
    return res[0][0], res[1][0], res[0][1], res[1][1]


def _col_blocks(s):
    return [s[:, c * LANES:(c + 1) * LANES] for c in range(s.shape[1] // LANES)]


def _block_max(s):
    blocks = _col_blocks(s)
    mc = blocks[0]
    for b in blocks[1:]:
        mc = jnp.maximum(mc, b)
    return jnp.max(mc, axis=-1, keepdims=True)


def _flash_update(s, m_ref, acc_ref, idx, v1):
    m_old = m_ref[idx]
    m_new = jnp.maximum(m_old, _block_max(s))
    alpha = jnp.exp2(m_old - m_new)
    p = jnp.concatenate([_bf(jnp.exp2(b - m_new)) for b in _col_blocks(s)], axis=1)
    reps = acc_ref.shape[-1] // LANES
    acc_ref[idx] = jnp.concatenate([alpha] * reps, axis=1) * acc_ref[idx] + _dot(p, v1)
    m_ref[idx] = m_new


def _causal_bias(tq, tk):
    r = lax.broadcasted_iota(jnp.int32, (tq, tk), 0)
    c = lax.broadcasted_iota(jnp.int32, (tq, tk), 1)
    return jnp.where(r >= c, 0.0, NEG_BIG).astype(F32)


def _mla_prompt_kernel(q_ref, kv_ref, wvu_ref, o_ref, m_ref, l_ref, acc_ref):
    i, j = pl.program_id(1), pl.program_id(2)
    tq = q_ref.shape[0]
    qw = KV_LORA + LANES

    @pl.when(j == 0)
    def _():
        m_ref[...] = jnp.full_like(m_ref, NEG_BIG)
        l_ref[...] = jnp.zeros_like(l_ref)
        acc_ref[...] = jnp.zeros_like(acc_ref)

    def step(masked):
        kv = kv_ref[...]
        bias = _causal_bias(tq, tq) if masked else None
        for h in range(H_B):
            s = _dot_nt(q_ref[:, h * qw:(h + 1) * qw], kv)
            if masked:
                s = s + bias
            m_old = m_ref[h]
            m_new = jnp.maximum(m_old, _block_max(s))
            alpha = jnp.exp2(m_old - m_new)
            blocks = [jnp.exp2(b - m_new) for b in _col_blocks(s)]
            lsum = blocks[0]
            for b in blocks[1:]:
                lsum = lsum + b
            l_ref[h] = alpha * l_ref[h] + lsum
            p = jnp.concatenate([_bf(b) for b in blocks], axis=1)
            acc_ref[h] = jnp.concatenate([alpha] * (KV_LORA // LANES), axis=1) * acc_ref[h] + _dot(p, kv[:, 0:KV_LORA])
            m_ref[h] = m_new

    @pl.when(j < i)
    def _():
        step(False)

    @pl.when(j == i)
    def _():
        step(True)
        for h in range(H_B):
            lat = acc_ref[h] / jnp.sum(l_ref[h], axis=-1, keepdims=True)
            o_ref[:, h * V_B:(h + 1) * V_B] = _bf(_dot(_bf(lat), wvu_ref[h]))


def _mla_prompt(qm, kvm, w_vu, b_prompt, t_len, tq):
    n_q = t_len // tq
    qw = H_B * (KV_LORA + LANES)
    return pl.pallas_call(
        _mla_prompt_kernel,
        grid=(b_prompt, n_q, n_q),
        in_specs=[pl.BlockSpec((tq, qw), lambda b, i, j: (b * n_q + i, 0)),
                  pl.BlockSpec((tq, KV_LORA + LANES), lambda b, i, j: (b * n_q + jnp.minimum(i, j), 0)),
                  pl.BlockSpec(w_vu.shape, lambda b, i, j: (0, 0, 0))],
        out_specs=pl.BlockSpec((tq, H_B * V_B), lambda b, i, j: (b * n_q + i, 0)),
        out_shape=jax.ShapeDtypeStruct((b_prompt * t_len, H_B * V_B), BF16),
        scratch_shapes=[pltpu.VMEM((H_B, tq, LANES), F32), pltpu.VMEM((H_B, tq, LANES), F32),
                        pltpu.VMEM((H_B, tq, KV_LORA), F32)],
        compiler_params=_params("parallel", "parallel", "arbitrary"),
        name="mla_prompt",
    )(qm, kvm, w_vu)


def _lambda(lam_ref, lam_init):
    l = lam_ref[...]
    a = jnp.sum(l[0:1] * l[1:2], axis=-1, keepdims=True)
    b = jnp.sum(l[2:3] * l[3:4], axis=-1, keepdims=True)
    return jnp.exp(a) - jnp.exp(b) + lam_init


def _diff_prompt_kernel(q_ref, k_ref, v_ref, lam_ref, gsub_ref, o_ref, m_ref, acc_ref, *, lam_init):
    i, j = pl.program_id(1), pl.program_id(2)
    tq = q_ref.shape[0]

    @pl.when(j == 0)
    def _():
        m_ref[...] = jnp.full_like(m_ref, NEG_BIG)
        acc_ref[...] = jnp.zeros_like(acc_ref)

    def step(masked):
        bias = _causal_bias(tq, tq) if masked else None
        ones = jnp.ones((tq, LANES), BF16)
        for g in range(KV_C):
            kg = k_ref[:, g * LANES:(g + 1) * LANES]
            v1 = jnp.concatenate([v_ref[:, g * LANES:(g + 1) * LANES], ones], axis=1)
            for rs in range(2 * REP_C):
                idx = g * 2 * REP_C + rs
                s = _dot_nt(q_ref[:, idx * LANES:(idx + 1) * LANES], kg)
                if masked:
                    s = s + bias
                _flash_update(s, m_ref, acc_ref, idx, v1)

    @pl.when(j < i)
    def _():
        step(False)

    @pl.when(j == i)
    def _():
        step(True)
        lam = _lambda(lam_ref, lam_init)
        for gr in range(KV_C * REP_C):
            a1, a2 = acc_ref[2 * gr], acc_ref[2 * gr + 1]
            o = a1[:, 0:LANES] / a1[:, LANES:2 * LANES] - lam * (a2[:, 0:LANES] / a2[:, LANES:2 * LANES])
            o_ref[:, gr * LANES:(gr + 1) * LANES] = _bf(_rms(o) * gsub_ref[...] * (1.0 - lam_init))


def _diff_prompt(qdm, kcm, vcm, lam4, g_sub, lam_init, b_prompt, t_len, tq):
    n_q = t_len // tq
    n_maps = KV_C * REP_C * 2
    return pl.pallas_call(
        functools.partial(_diff_prompt_kernel, lam_init=lam_init),
        grid=(b_prompt, n_q, n_q),
        in_specs=[pl.BlockSpec((tq, n_maps * LANES), lambda b, i, j: (b * n_q + i, 0)),
                  pl.BlockSpec((tq, KV_C * LANES), lambda b, i, j: (b * n_q + jnp.minimum(i, j), 0)),
                  pl.BlockSpec((tq, KV_C * LANES), lambda b, i, j: (b * n_q + jnp.minimum(i, j), 0)),
                  pl.BlockSpec(lam4.shape, lambda b, i, j: (0, 0)),
                  pl.BlockSpec(g_sub.shape, lambda b, i, j: (0, 0))],
        out_specs=pl.BlockSpec((tq, H_C * 2 * D_C), lambda b, i, j: (b * n_q + i, 0)),
        out_shape=jax.ShapeDtypeStruct((b_prompt * t_len, H_C * 2 * D_C), BF16),
        scratch_shapes=[pltpu.VMEM((n_maps, tq, LANES), F32), pltpu.VMEM((n_maps, tq, 2 * LANES), F32)],
        compiler_params=_params("parallel", "parallel", "arbitrary"),
        name="diff_prompt",
    )(qdm, kcm, vcm, lam4, g_sub)


def _page_copies(pt_ref, caches, bufs, sems, layer, step, slot, pg, n_groups):
    b = step // n_groups
    p0 = (step % n_groups) * pg
    out = []
    for p in range(pg):
        page = pt_ref[b, p0 + p]
        for cache, buf, sem in zip(caches, bufs, sems):
            out.append(pltpu.make_async_copy(cache.at[layer, page], buf.at[slot, p], sem.at[slot]))
    return out


def _prefetch_pages(pt_ref, caches, bufs, sems, layer, pg, n_groups):
    step = pl.program_id(0) * n_groups + pl.program_id(1)
    n_steps = pl.num_programs(0) * n_groups
    slot = step % 2

    @pl.when(step == 0)
    def _():
        for cp in _page_copies(pt_ref, caches, bufs, sems, layer, step, slot, pg, n_groups):
            cp.start()

    @pl.when(step + 1 < n_steps)
    def _():
        for cp in _page_copies(pt_ref, caches, bufs, sems, layer, step + 1, 1 - slot, pg, n_groups):
            cp.start()

    for cp in _page_copies(pt_ref, caches, bufs, sems, layer, step, slot, pg, n_groups):
        cp.wait()
    return slot


def _decode_update(s_parts, v_parts, tail, m_ref, l_ref, acc_ref):
    m_old = m_ref[...]
    mx = _block_max(s_parts[0])
    for sp in s_parts[1:]:
        mx = jnp.maximum(mx, _block_max(sp))
    if tail is not None:
        mx = jnp.maximum(mx, jnp.max(tail[0], axis=-1, keepdims=True))
    m_new = jnp.maximum(m_old, mx)
    alpha = jnp.exp2(m_old - m_new)
    lsum, pvs = None, [None, None]
    for idx, (sp, vp) in enumerate(zip(s_parts, v_parts)):
        blocks = [jnp.exp2(b - m_new) for b in _col_blocks(sp)]
        for b in blocks:
            lsum = b if lsum is None else lsum + b
        p = jnp.concatenate([_bf(b) for b in blocks], axis=1)
        if vp.shape[1] > LANES:
            d = jnp.concatenate([_dot(p, vp[:, c * LANES:(c + 1) * LANES]) for c in range(vp.shape[1] // LANES)], axis=1)
        else:
            d = _dot(p, vp)
        pvs[idx % 2] = d if pvs[idx % 2] is None else pvs[idx % 2] + d
    pv = pvs[0] if pvs[1] is None else pvs[0] + pvs[1]
    lrow = jnp.sum(lsum, axis=-1, keepdims=True)
    if tail is not None:
        pt = jnp.exp2(tail[0] - m_new[:, 0:1])
        lrow = lrow + jnp.sum(pt, axis=-1, keepdims=True)
        pv = pv + _dot(_bf(pt), tail[1])
    l_ref[...] = alpha * l_ref[...] + lrow
    acc_ref[...] = jnp.concatenate([alpha] * (acc_ref.shape[-1] // LANES), axis=1) * acc_ref[...] + pv
    m_ref[...] = m_new


def _mla_decode_kernel(pt_ref, q_ref, kvn_ref, wvu_ref, ckv_hbm, krt_hbm, o_ref,
                       ckv_buf, krt_buf, sem_c, sem_r, m_ref, l_ref, acc_ref, *, layer, pg, n_groups, pc):
    gi = pl.program_id(1)
    slot = _prefetch_pages(pt_ref, (ckv_hbm, krt_hbm), (ckv_buf, krt_buf), (sem_c, sem_r), layer, pg, n_groups)
    page = ckv_buf.shape[2]
    ts = q_ref.shape[0] // H_B

    @pl.when(gi == 0)
    def _():
        m_ref[...] = jnp.full_like(m_ref, NEG_BIG)
        l_ref[...] = jnp.zeros_like(l_ref)
        acc_ref[...] = jnp.zeros_like(acc_ref)

    def step(last):
        q = q_ref[...]
        q_lat, q_rope = q[:, 0:KV_LORA], q[:, KV_LORA:KV_LORA + ROPE_B]
        s_parts, v_parts = [], []
        for ch in range(pg // pc):
            kc = _bf(ckv_buf[slot, ch * pc:(ch + 1) * pc].reshape(pc * page, KV_LORA))
            krt = jnp.concatenate([_bf(krt_buf[slot, p]) for p in range(ch * pc, (ch + 1) * pc)], axis=1)
            s_parts.append(_dot_nt(q_lat, kc) + _dot(q_rope, krt))
            v_parts.append(kc)
        tail = None
        if last:
            kvn = kvn_ref[...]
            s = _dot_nt(q[:, 0:KV_LORA + ROPE_B], kvn[:, 0:KV_LORA + ROPE_B])
            r = lax.broadcasted_iota(jnp.int32, s.shape, 0) % ts
            c = lax.broadcasted_iota(jnp.int32, s.shape, 1)
            tail = (jnp.where(r >= c, s, NEG_BIG), kvn[:, 0:KV_LORA])
        _decode_update(s_parts, v_parts, tail, m_ref, l_ref, acc_ref)

    if n_groups > 1:
        @pl.when(gi < n_groups - 1)
        def _():
            step(False)

    @pl.when(gi == n_groups - 1)
    def _():
        step(True)
        lat = acc_ref[...] / l_ref[:, 0:1]
        for h in range(H_B):
            o_ref[:, h * V_B:(h + 1) * V_B] = _bf(_dot(_bf(lat[h * ts:(h + 1) * ts]), wvu_ref[h]))


def _mla_decode(page_table, q_s, kv_new, w_vu, cache_ckv, cache_kr, layer, pg, pc):
    bs, n_pages = page_table.shape
    rows = q_s.shape[1]
    ts = rows // H_B
    page = cache_ckv.shape[2]
    n_groups = n_pages // pg
    kern = functools.partial(_mla_decode_kernel, layer=layer, pg=pg, n_groups=n_groups, pc=pc)
    return pl.pallas_call(
        kern,
        grid_spec=pltpu.PrefetchScalarGridSpec(
            num_scalar_prefetch=1,
            grid=(bs, n_groups),
            in_specs=[pl.BlockSpec((None, rows, KV_LORA + LANES), lambda b, g, pt: (b, 0, 0)),
                      pl.BlockSpec((None, ts, KV_LORA + LANES), lambda b, g, pt: (b, 0, 0)),
                      pl.BlockSpec(w_vu.shape, lambda b, g, pt: (0, 0, 0)),
                      pl.BlockSpec(memory_space=pl.ANY), pl.BlockSpec(memory_space=pl.ANY)],
            out_specs=pl.BlockSpec((None, ts, H_B * V_B), lambda b, g, pt: (b, 0, 0)),
            scratch_shapes=[pltpu.VMEM((2, pg, page, KV_LORA), F32), pltpu.VMEM((2, pg, ROPE_B, page), F32),
                            pltpu.SemaphoreType.DMA((2,)), pltpu.SemaphoreType.DMA((2,)),
                            pltpu.VMEM((rows, LANES), F32), pltpu.VMEM((rows, LANES), F32),
                            pltpu.VMEM((rows, KV_LORA), F32)]),
        out_shape=jax.ShapeDtypeStruct((bs, ts, H_B * V_B), BF16),
        compiler_params=_params("arbitrary", "arbitrary"),
        name="mla_decode",
    )(page_table, q_s, kv_new, w_vu, cache_ckv, cache_kr)


def _diff_decode_kernel(pt_ref, q_ref, kn_ref, vn_ref, lam_ref, gsub_ref, k_hbm, v_hbm, o_ref,
                        k_buf, v_buf, sem_k, sem_v, m_ref, l_ref, acc_ref, *, layer, pg, n_groups, pu, pc, lam_init):
    gi = pl.program_id(1)
    slot = _prefetch_pages(pt_ref, (k_hbm, v_hbm), (k_buf, v_buf), (sem_k, sem_v), layer, pg, n_groups)
    prow = k_buf.shape[2]
    rows = q_ref.shape[0]
    ts = rows // (KV_C * REP_C * 2)

    @pl.when(gi == 0)
    def _():
        m_ref[...] = jnp.full_like(m_ref, NEG_BIG)
        l_ref[...] = jnp.zeros_like(l_ref)
        acc_ref[...] = jnp.zeros_like(acc_ref)

    def group_match(n_cols):
        rg = lax.broadcasted_iota(jnp.int32, (rows, n_cols), 0) // (rows // KV_C)
        cg = lax.broadcasted_iota(jnp.int32, (rows, n_cols), 1) % KV_C
        return rg == cg

    def step(last):
        q = q_ref[...]
        same = group_match(pc * prow)
        for u in range(pg // pu):
            s_parts, v_parts = [], []
            for ch in range(u * (pu // pc), (u + 1) * (pu // pc)):
                kk = _bf(k_buf[slot, ch * pc:(ch + 1) * pc].reshape(pc * prow, LANES))
                s_parts.append(jnp.where(same, _dot_nt(q, kk), NEG_BIG))
                v_parts.append(_bf(v_buf[slot, ch * pc:(ch + 1) * pc].reshape(pc * prow, LANES)))
            tail = None
            if last and u == pg // pu - 1:
                s = _dot_nt(q, kn_ref[...])
                r = lax.broadcasted_iota(jnp.int32, s.shape, 0) % ts
                c = lax.broadcasted_iota(jnp.int32, s.shape, 1) // KV_C
                tail = (jnp.where(group_match(ts * KV_C) & (r >= c), s, NEG_BIG), vn_ref[...])
            _decode_update(s_parts, v_parts, tail, m_ref, l_ref, acc_ref)

    if n_groups > 1:
        @pl.when(gi < n_groups - 1)
        def _():
            step(False)

    @pl.when(gi == n_groups - 1)
    def _():
        step(True)
        o = acc_ref[...] / l_ref[:, 0:1]
        lam = _lambda(lam_ref, lam_init)
        for gr in range(KV_C * REP_C):
            o1 = o[(2 * gr) * ts:(2 * gr + 1) * ts]
            o2 = o[(2 * gr + 1) * ts:(2 * gr + 2) * ts]
            o_ref[:, gr * LANES:(gr + 1) * LANES] = _bf(_rms(o1 - lam * o2) * gsub_ref[...] * (1.0 - lam_init))


def _diff_decode(page_table, q_s, k_new, v_new, lam4, g_sub, cache_k, cache_v, layer, lam_init, pg, pu, pc):
    bs, n_pages = page_table.shape
    rows = q_s.shape[1]
    ts = rows // (KV_C * REP_C * 2)
    prow = cache_k.shape[2]
    n_groups = n_pages // pg
    kern = functools.partial(_diff_decode_kernel, layer=layer, pg=pg, n_groups=n_groups, pu=pu, pc=pc,
                             lam_init=lam_init)
    return pl.pallas_call(
        kern,
        grid_spec=pltpu.PrefetchScalarGridSpec(
            num_scalar_prefetch=1,
            grid=(bs, n_groups),
            in_specs=[pl.BlockSpec((None, rows, LANES), lambda b, g, pt: (b, 0, 0)),
                      pl.BlockSpec((None, ts * KV_C, LANES), lambda b, g, pt: (b, 0, 0)),
                      pl.BlockSpec((None, ts * KV_C, LANES), lambda b, g, pt: (b, 0, 0)),
                      pl.BlockSpec(lam4.shape, lambda b, g, pt: (0, 0)),
                      pl.BlockSpec(g_sub.shape, lambda b, g, pt: (0, 0)),
                      pl.BlockSpec(memory_space=pl.ANY), pl.BlockSpec(memory_space=pl.ANY)],
            out_specs=pl.BlockSpec((None, ts, H_C * 2 * D_C), lambda b, g, pt: (b, 0, 0)),
            scratch_shapes=[pltpu.VMEM((2, pg, prow, LANES), F32), pltpu.VMEM((2, pg, prow, LANES), F32),
                            pltpu.SemaphoreType.DMA((2,)), pltpu.SemaphoreType.DMA((2,)),
                            pltpu.VMEM((rows, LANES), F32), pltpu.VMEM((rows, LANES), F32),
                            pltpu.VMEM((rows, LANES), F32)]),
        out_shape=jax.ShapeDtypeStruct((bs, ts, H_C * 2 * D_C), BF16),
        compiler_params=_params("arbitrary", "arbitrary"),
        name="diff_decode",
    )(page_table, q_s, k_new, v_new, lam4, g_sub, cache_k, cache_v)


def _layernorm(z, g, b):
    mu = jnp.mean(z, axis=-1, keepdims=True)
    d = z - mu
    var = jnp.mean(d * d, axis=-1, keepdims=True)
    return d * lax.rsqrt(var + EPS) * g + b


def _split_rows(tm, n_pt):
    return (lambda i, *_: (jnp.minimum(i, n_pt - 1), 0)), (lambda i, *_: (jnp.maximum(i - n_pt, 0), 0))


def _out_proj_kernel(o1p_ref, o1s_ref, o2p_ref, o2s_ref, w1_ref, w2_ref, x_ref, g_ref, b_ref, y_ref, *, n_pt):
    def body(o1_ref, o2_ref):
        y = _dot(o1_ref[...], w1_ref[...]) + _dot(o2_ref[...], w2_ref[...])
        y_ref[...] = _layernorm(ALPHA * x_ref[...] + y, g_ref[...], b_ref[...])

    @pl.when(pl.program_id(0) < n_pt)
    def _():
        body(o1p_ref, o2p_ref)

    @pl.when(pl.program_id(0) >= n_pt)
    def _():
        body(o1s_ref, o2s_ref)


def _out_proj(o1p, o1s, o2p, o2s, w_out, layer, x, g, b, tm):
    n = x.shape[0]
    half = o1p.shape[1]
    n_pt = o1p.shape[0] // tm
    pix, six = _split_rows(tm, n_pt)
    row = lambda w: pl.BlockSpec((tm, w), lambda i: (i, 0))
    vec = pl.BlockSpec((1, D_MODEL), lambda i: (0, 0))
    return pl.pallas_call(
        functools.partial(_out_proj_kernel, n_pt=n_pt),
        grid=(n // tm,),
        in_specs=[pl.BlockSpec((tm, half), pix), pl.BlockSpec((tm, half), six),
                  pl.BlockSpec((tm, half), pix), pl.BlockSpec((tm, half), six),
                  pl.BlockSpec((None, half, D_MODEL), lambda i: (layer, 0, 0)),
                  pl.BlockSpec((None, half, D_MODEL), lambda i: (layer, 1, 0)), row(D_MODEL), vec, vec],
        out_specs=row(D_MODEL),
        out_shape=jax.ShapeDtypeStruct((n, D_MODEL), F32),
        compiler_params=_params("arbitrary"),
        name="out_proj_ln",
    )(o1p, o1s, o2p, o2s, w_out, w_out, x, g, b)


def _mlp_kernel(x_ref, w1_ref, w2_ref, g_ref, b_ref, *outs, n_pt, n_sub):
    sub = x_ref.shape[0] // n_sub
    ys = []
    for u in range(n_sub):
        x = x_ref[u * sub:(u + 1) * sub, :]
        h = jnp.maximum(_dot(_bf(x), w1_ref[...]), 0.0)
        y = _dot(_bf(h * h), w2_ref[...])
        ys.append(_layernorm(ALPHA * x + y, g_ref[...], b_ref[...]))
    y = ys[0] if n_sub == 1 else jnp.concatenate(ys, axis=0)
    if n_pt is None:
        outs[0][...] = y
    else:
        _store_split(n_pt, outs[0], outs[1], y)


def _mlp(x, w1, w2, layer, g, b, tm, n_prompt=None):
    n = x.shape[0]
    vec = pl.BlockSpec((1, D_MODEL), lambda i: (0, 0))
    if n_prompt is None:
        n_pt = None
        out_specs = pl.BlockSpec((tm, D_MODEL), lambda i: (i, 0))
        out_shape = jax.ShapeDtypeStruct((n, D_MODEL), F32)
    else:
        n_pt = n_prompt // tm
        pix, six = _split_rows(tm, n_pt)
        out_specs = [pl.BlockSpec((tm, D_MODEL), pix), pl.BlockSpec((tm, D_MODEL), six)]
        out_shape = [jax.ShapeDtypeStruct((n_prompt, D_MODEL), F32), jax.ShapeDtypeStruct((n - n_prompt, D_MODEL), F32)]
    once = pl.Buffered(1)
    return pl.pallas_call(
        functools.partial(_mlp_kernel, n_pt=n_pt, n_sub=2 if tm % 16 == 0 else 1),
        grid=(n // tm,),
        in_specs=[pl.BlockSpec((tm, D_MODEL), lambda i: (i, 0)),
                  pl.BlockSpec((None, D_MODEL, D_FF), lambda i: (layer, 0, 0), pipeline_mode=once),
                  pl.BlockSpec((None, D_FF, D_MODEL), lambda i: (layer, 0, 0), pipeline_mode=once), vec, vec],
        out_specs=out_specs,
        out_shape=out_shape,
        compiler_params=_params("arbitrary"),
        name="mlp_ln",
    )(x, w1, w2, g, b)


def _pad_cols(w, to):
    return jnp.pad(w, ((0, 0), (0, to - w.shape[1])))


def _prep_even(w_in, w_qu, w_ku):
    w_in_p = _bf(_pad_cols(w_in, 2304))
    nope = w_qu[:, :, :NOPE_B].reshape(Q_LORA, H_B * NOPE_B)
    rope = jnp.pad(w_qu[:, :, NOPE_B:], ((0, 0), (0, 0), (0, LANES - ROPE_B))).reshape(Q_LORA, H_B * LANES)
    w_qu_p = _bf(jnp.concatenate([nope, rope], -1))
    w_ku_t = _bf(jnp.transpose(w_ku, (1, 2, 0)))
    return w_in_p, w_qu_p, w_ku_t


def _prep_odd(w_in, w_gu):
    return _bf(_pad_cols(w_in, 2688)), _bf(jnp.pad(w_gu, ((0, LANES - GK_RANK), (0, 0))))


def kernel(x_prompt, x_sample, state_ret, cache_mla_ckv, cache_mla_krope, cache_diff_k, cache_diff_v, state_gla, page_table, w_in_even, g_q_lora, g_kv_lora, w_q_up, w_k_up, w_v_up, w_out_even, w_in_odd, lam_q1, lam_k1, lam_q2, lam_k2, g_subln, w_gate_up, b_gate_up, g_gla_norm, w_out_odd, ln1_g, ln1_b, ln2_g, ln2_b, w_ff1, w_ff2):
    bp, tp, _ = x_prompt.shape
    bs, ts, _ = x_sample.shape
    n_p, n_s = bp * tp, bs * ts
    n_pages, page = page_table.shape[1], cache_mla_ckv.shape[2]
    past_len = n_pages * page
    assert LIN_TILE % ts == 0 and n_s % LIN_TILE == 0 and tp % LIN_TILE == 0

    tm = _tile(math.gcd(tp, n_s), 512)
    tq = _tile(tp, TQ_PREF)
    pg = _tile(n_pages, 64)
    pg_mla = pg
    pu = _tile(pg, 32)
    pc = _tile(pu, 8)

    pos = jnp.concatenate([jnp.arange(tp, dtype=jnp.int32), past_len + jnp.tile(jnp.arange(ts, dtype=jnp.int32), bs)])
    tab_ret = _rope_table(pos, RET_THETA, DK_A, DK_A, 2)
    tab_mla = _rope_table(pos, ROPE_THETA, ROPE_B, ROPE_B, 1)
    tab_diff = _rope_table(pos, ROPE_THETA, ROT_C, D_C, 2)

    cache_krt = jnp.swapaxes(cache_mla_krope, 2, 3)
    cache_k = cache_diff_k.reshape(cache_diff_k.shape[:2] + (page * KV_C, 2 * D_C))
    cache_v = cache_diff_v.reshape(cache_diff_v.shape[:2] + (page * KV_C, 2 * D_C))

    x = jnp.concatenate([x_prompt.reshape(n_p, D_MODEL), x_sample.reshape(n_s, D_MODEL)], 0)
    row2 = lambda v: v.reshape(1, -1)
    w_ff1_b, w_ff2_b, w_out_even_b, w_out_odd_b = _bf(w_ff1), _bf(w_ff2), _bf(w_out_even), _bf(w_out_odd)
    outs = {k: [] for k in ("p_ret", "p_ckv", "p_kr", "p_dk", "p_dv", "p_gla", "s_ret", "s_ckv", "s_kr", "s_dk", "s_dv", "s_gla")}
    for l in range(DEPTH):
        j = l // 2
        if l % 2 == 0:
            w_in_p, w_qu_p, w_ku_t = _prep_even(w_in_even[j], w_q_up[j], w_k_up[j])
            w_vu = _bf(jnp.transpose(w_v_up[j], (1, 0, 2)))
            qa, ka, va, ga, qm, kvm, ckv_p, ckv_s, kr_p, kr_s = _proj_even(
                x, w_in_p, tab_ret, tab_mla, row2(g_q_lora[j]), row2(g_kv_lora[j]), w_qu_p, w_ku_t, tp, n_p, tm)
            o_ap, o_as, st_p, st_s = _lin_attn(qa, ka, va, ga, None, None, state_ret, j, gla=False, n_prompt=n_p,
                                               t_len=tp, b_prompt=bp, c_s=ts)
            o_bp = _mla_prompt(qm, kvm, w_vu, bp, tp, tq)
            qw = KV_LORA + LANES
            q_s = qm[n_p:].reshape(bs, ts, H_B, qw).transpose(0, 2, 1, 3).reshape(bs, H_B * ts, qw)
            o_bs = _mla_decode(page_table, q_s, kvm[n_p:].reshape(bs, ts, qw), w_vu, cache_mla_ckv, cache_krt,
                               j, pg_mla, pc)
            x = _out_proj(o_ap, o_as, o_bp, o_bs.reshape(n_s, H_B * V_B), w_out_even_b, j, x, row2(ln1_g[l]),
                          row2(ln1_b[l]), tm)
            outs["p_ret"].append(st_p)
            outs["s_ret"].append(st_s)
            outs["p_ckv"].append(ckv_p.reshape(bp, tp, KV_LORA))
            outs["s_ckv"].append(ckv_s.reshape(bs, ts, KV_LORA))
            outs["p_kr"].append(kr_p.reshape(bp, tp, ROPE_B))
            outs["s_kr"].append(kr_s.reshape(bs, ts, ROPE_B))
        else:
            lam_init = 0.8 - 0.6 * math.exp(-0.3 * l)
            w_in_p, w_gu_p = _prep_odd(w_in_odd[j], w_gate_up[j])
            lam4 = jnp.stack([lam_q1[j], lam_k1[j], lam_q2[j], lam_k2[j]])
            g_sub = row2(g_subln[j])
            qdm, kcm, vcm, qd, kd, vd, gd, la, kc_p, kc_s, vc_p, vc_s = _proj_odd(
                x, w_in_p, tab_diff, w_gu_p, row2(b_gate_up[j]), tp, n_p, tm)
            o_dp, o_ds, st_p, st_s = _lin_attn(qd, kd, vd, gd, la, row2(g_gla_norm[j]), state_gla, j, gla=True,
                                               n_prompt=n_p, t_len=tp, b_prompt=bp, c_s=ts)
            o_cp = _diff_prompt(qdm, kcm, vcm, lam4, g_sub, lam_init, bp, tp, tq)
            n_maps = KV_C * REP_C * 2
            q_s = qdm[n_p:].reshape(bs, ts, n_maps, LANES).transpose(0, 2, 1, 3).reshape(bs, n_maps * ts, LANES)
            k_new = kcm[n_p:].reshape(bs, ts * KV_C, LANES)
            v_new = vcm[n_p:].reshape(bs, ts * KV_C, LANES)
            o_cs = _diff_decode(page_table, q_s, k_new, v_new, lam4, g_sub, cache_k, cache_v, j, lam_init, pg, pu, pc)
            x = _out_proj(o_cp, o_cs.reshape(n_s, H_C * 2 * D_C), o_dp, o_ds, w_out_odd_b, j, x, row2(ln1_g[l]),
                          row2(ln1_b[l]), tm)
            outs["p_dk"].append(kc_p.reshape(bp, tp, KV_C, 2 * D_C))
            outs["s_dk"].append(kc_s.reshape(bs, ts, KV_C, 2 * D_C))
            outs["p_dv"].append(vc_p.reshape(bp, tp, KV_C, 2 * D_C))
            outs["s_dv"].append(vc_s.reshape(bs, ts, KV_C, 2 * D_C))
            outs["p_gla"].append(st_p)
            outs["s_gla"].append(st_s)
        x = _mlp(x, w_ff1_b, w_ff2_b, l, row2(ln2_g[l]), row2(ln2_b[l]), tm,
                 n_prompt=n_p if l == DEPTH - 1 else None)
    st = lambda k: jnp.stack(outs[k])
    return (x[0].reshape(bp, tp, D_MODEL), x[1].reshape(bs, ts, D_MODEL),
            st("p_ret"), st("p_ckv"), st("p_kr"), st("p_dk"), st("p_dv"), st("p_gla"),
            st("s_ret"), st("s_ckv"), st("s_kr"), st("s_dk"), st("s_dv"), st("s_gla"))
```

```python
import functools
import math

import numpy as np
import jax
import jax.numpy as jnp
from jax import lax
from jax.experimental import pallas as pl
from jax.experimental.pallas import tpu as pltpu

F32 = jnp.float32
BF16 = jnp.bfloat16

D_MODEL = 1024
DEPTH = 4
H_A, DK_A, DV_A = 4, 64, 128
RET_THETA = 10000.0
H_B, Q_LORA, KV_LORA, NOPE_B, ROPE_B, V_B = 4, 384, 256, 128, 64, 128
H_C, KV_C, D_C = 4, 2, 64
REP_C = H_C // KV_C
ROT_C = D_C // 4
H_D, DK_D, DV_D = 4, 64, 128
GK_RANK = 16
GATE_NORM = 16.0
D_FF = 4 * D_MODEL
ROPE_THETA = 500000.0
ALPHA = (2 * DEPTH) ** 0.25
EPS = 1e-5
EVEN_SPLITS = (H_A * DK_A, H_A * DK_A, H_A * DV_A, H_A * DV_A, Q_LORA, KV_LORA, ROPE_B)
ODD_SPLITS = (H_C * 2 * D_C, KV_C * 2 * D_C, KV_C * 2 * D_C, H_D * DK_D, H_D * DK_D, H_D * DV_D, H_D * DV_D, GK_RANK)

LANES = 128
LIN_TILE = 64
NEG_BIG = -1e30
TQ_PREF = 512
LOG2E = math.log2(math.e)
QSCALE_MLA = (NOPE_B + ROPE_B) ** -0.5 * LOG2E
QSCALE_DIFF = D_C ** -0.5 * LOG2E
VMEM_LIMIT = 52 * 1024 * 1024
FUSED_VMEM_LIMIT = 58 * 1024 * 1024


def _dot(a, b):
    return jnp.dot(a, b, preferred_element_type=F32)


def _dot_nt(a, b):
    return lax.dot_general(a, b, (((1,), (1,)), ((), ())), preferred_element_type=F32)


def _dot_tn(a, b):
    return lax.dot_general(a, b, (((0,), (0,)), ((), ())), preferred_element_type=F32)


def _bf(x):
    return x.astype(BF16)


def _rms(x):
    return x * lax.rsqrt(jnp.mean(x * x, axis=-1, keepdims=True) + EPS)


def _silu(x):
    return x / (1.0 + jnp.exp(-x))


def _tile(n, pref):
    t = min(n, pref)
    while n % t:
        t -= 8
    return t


def _params(*sem):
    return pltpu.CompilerParams(dimension_semantics=sem, vmem_limit_bytes=VMEM_LIMIT)


def _rope_table(pos, theta, rot, group, n_groups):
    half = rot // 2
    inv = theta ** (-np.arange(half, dtype=np.float64) / half)
    ang = pos.astype(np.float64)[:, None] * inv[None, :]
    cos, sin = np.cos(ang), np.sin(ang)
    n = pos.shape[0]
    one = np.ones((n, group - rot))
    zh = np.zeros((n, half))
    zr = np.zeros((n, group - rot))
    c = np.concatenate([cos, cos, one], -1)
    s1 = np.concatenate([-sin, zh, zr], -1)
    s2 = np.concatenate([zh, sin, zr], -1)
    pad = np.zeros((n, LANES - n_groups * group))
    cat = lambda t: np.concatenate([t] * n_groups + [pad], -1)
    return jnp.asarray(np.concatenate([cat(c), cat(s1), cat(s2)], -1), F32)


def _rope_slab(x, tab, half):
    c, s1, s2 = tab[:, 0:LANES], tab[:, LANES:2 * LANES], tab[:, 2 * LANES:3 * LANES]
    return x * c + pltpu.roll(x, LANES - half, 1) * s1 + pltpu.roll(x, half, 1) * s2


def _store_split(n_pt, p_ref, s_ref, val):
    @pl.when(pl.program_id(0) < n_pt)
    def _():
        p_ref[...] = val

    @pl.when(pl.program_id(0) >= n_pt)
    def _():
        s_ref[...] = val


def _proj_even_kernel(x_ref, w_ref, taba_ref, tabb_ref, gq_ref, gkv_ref, wqu_ref, wku_ref,
                      qa_ref, ka_ref, va_ref, ga_ref, qm_ref, kvm_ref, ckvp_ref, ckvs_ref, krp_ref, krs_ref, *, n_pt):
    h = _dot(_bf(x_ref[...]), w_ref[...])
    taba = taba_ref[...]
    tabb = tabb_ref[...]
    for j in range(2):
        sl = slice(j * LANES, (j + 1) * LANES)
        qa_ref[:, sl] = _rope_slab(h[:, j * LANES:(j + 1) * LANES], taba, DK_A // 2)
        ka_ref[:, sl] = _rope_slab(h[:, 256 + j * LANES:256 + (j + 1) * LANES], taba, DK_A // 2) * (DK_A ** -0.5)
    va_ref[...] = _bf(h[:, 512:1024])
    ga_ref[...] = h[:, 1024:1536]
    cqn = _rms(h[:, 1536:1920]) * gq_ref[...]
    ckvn = _rms(h[:, 1920:2176]) * gkv_ref[...]
    kr = _rope_slab(h[:, 2176:2304], tabb, ROPE_B // 2)
    _store_split(n_pt, ckvp_ref, ckvs_ref, ckvn)
    _store_split(n_pt, krp_ref, krs_ref, kr[:, 0:ROPE_B])
    kvm_ref[:, 0:KV_LORA] = _bf(ckvn)
    kvm_ref[:, KV_LORA:KV_LORA + LANES] = _bf(kr)
    q = _dot(_bf(cqn), wqu_ref[...])
    for hd in range(H_B):
        q_lat = _dot(_bf(q[:, hd * NOPE_B:(hd + 1) * NOPE_B]), wku_ref[hd])
        base = hd * (KV_LORA + LANES)
        qm_ref[:, base:base + KV_LORA] = _bf(q_lat * QSCALE_MLA)
        qr = _rope_slab(q[:, H_B * NOPE_B + hd * LANES:H_B * NOPE_B + (hd + 1) * LANES], tabb, ROPE_B // 2)
        qm_ref[:, base + KV_LORA:base + KV_LORA + LANES] = _bf(qr * QSCALE_MLA)


def _tab_index(tm, t_len, n_prompt):
    per = t_len // tm
    n_pt = n_prompt // tm
    return lambda i: (jnp.where(i < n_pt, i % per, per + i - n_pt), 0)


def _proj_even(x, w_in, taba, tabb, g_q, g_kv, w_qu, w_ku, t_len, n_prompt, tm):
    n = x.shape[0]
    row = lambda w: pl.BlockSpec((tm, w), lambda i: (i, 0))
    full = lambda a: pl.BlockSpec(a.shape, lambda i: (0,) * a.ndim)
    tix = _tab_index(tm, t_len, n_prompt)
    qw = H_B * (KV_LORA + LANES)
    n_pt = n_prompt // tm
    pix, six = _split_rows(tm, n_pt)
    outs = [(256, F32), (256, F32), (512, BF16), (512, F32), (qw, BF16), (KV_LORA + LANES, BF16)]
    split = [KV_LORA, ROPE_B]
    return pl.pallas_call(
        functools.partial(_proj_even_kernel, n_pt=n_pt),
        grid=(n // tm,),
        in_specs=[row(D_MODEL), full(w_in), pl.BlockSpec((tm, 3 * LANES), tix), pl.BlockSpec((tm, 3 * LANES), tix),
                  full(g_q), full(g_kv), full(w_qu), full(w_ku)],
        out_specs=[row(w) for w, _ in outs] + [pl.BlockSpec((tm, w), ix) for w in split for ix in (pix, six)],
        out_shape=[jax.ShapeDtypeStruct((n, w), d) for w, d in outs]
        + [jax.ShapeDtypeStruct((r, w), F32) for w in split for r in (n_prompt, n - n_prompt)],
        compiler_params=_params("arbitrary"),
        name="proj_even",
    )(x, w_in, taba, tabb, g_q, g_kv, w_qu, w_ku)


def _proj_odd_kernel(x_ref, w_ref, tabc_ref, wgu_ref, bgu_ref,
                     qdm_ref, kcm_ref, vcm_ref, qd_ref, kd_ref, vd_ref, gd_ref, la_ref,
                     kcp_ref, kcs_ref, vcp_ref, vcs_ref, *, n_pt):
    h = _dot(_bf(x_ref[...]), w_ref[...])
    tabc = tabc_ref[...]
    lane = lax.broadcasted_iota(jnp.int32, (1, LANES), 1)
    lo = (lane < D_C).astype(F32)
    hi = 1.0 - lo
    for j in range(4):
        qs = _rope_slab(h[:, j * LANES:(j + 1) * LANES], tabc, ROT_C // 2) * QSCALE_DIFF
        qdm_ref[:, (2 * j) * LANES:(2 * j + 1) * LANES] = _bf(qs * lo)
        qdm_ref[:, (2 * j + 1) * LANES:(2 * j + 2) * LANES] = _bf(qs * hi)
    kc = jnp.concatenate([_rope_slab(h[:, 512 + j * LANES:512 + (j + 1) * LANES], tabc, ROT_C // 2)
                          for j in range(KV_C)], axis=1)
    vc = h[:, 768:1024]
    _store_split(n_pt, kcp_ref, kcs_ref, kc)
    _store_split(n_pt, vcp_ref, vcs_ref, vc)
    kcm_ref[...] = _bf(kc)
    vcm_ref[...] = _bf(vc)
    qd_ref[...] = h[:, 1024:1280] * (DK_D ** -0.5)
    kd_ref[...] = h[:, 1280:1536]
    vd_ref[...] = _bf(h[:, 1536:2048])
    gd_ref[...] = h[:, 2048:2560]
    z = _dot(_bf(h[:, 2560:2688]), wgu_ref[...]) + bgu_ref[...]
    la_ref[...] = (jnp.minimum(z, 0.0) - jnp.log1p(jnp.exp(-jnp.abs(z)))) * (1.0 / GATE_NORM)


def _proj_odd(x, w_in, tabc, w_gu, b_gu, t_len, n_prompt, tm):
    n = x.shape[0]
    row = lambda w: pl.BlockSpec((tm, w), lambda i: (i, 0))
    full = lambda a: pl.BlockSpec(a.shape, lambda i: (0,) * a.ndim)
    tix = _tab_index(tm, t_len, n_prompt)
    n_pt = n_prompt // tm
    pix, six = _split_rows(tm, n_pt)
    outs = [(1024, BF16), (256, BF16), (256, BF16), (256, F32), (256, F32), (512, BF16), (512, F32), (256, F32)]
    split = [KV_C * 2 * D_C, KV_C * 2 * D_C]
    return pl.pallas_call(
        functools.partial(_proj_odd_kernel, n_pt=n_pt),
        grid=(n // tm,),
        in_specs=[row(D_MODEL), full(w_in), pl.BlockSpec((tm, 3 * LANES), tix), full(w_gu), full(b_gu)],
        out_specs=[row(w) for w, _ in outs] + [pl.BlockSpec((tm, w), ix) for w in split for ix in (pix, six)],
        out_shape=[jax.ShapeDtypeStruct((n, w), d) for w, d in outs]
        + [jax.ShapeDtypeStruct((r, w), F32) for w in split for r in (n_prompt, n - n_prompt)],
        compiler_params=_params("arbitrary"),
        name="proj_odd",
    )(x, w_in, tabc, w_gu, b_gu)


def _gla_consts(c):
    n = LIN_TILE
    i = np.arange(n)
    same = (i[:, None] // c) == (i[None, :] // c)
    mats = [same & (i[None, :] <= i[:, None]),
            same & (i[None, :] > i[:, None])]
    levels = []
    s = c // 2
    while s >= 1:
        levels.append(s)
        s //= 2
    masks = []
    for s in levels:
        blk, off = i // (2 * s), i % (2 * s)
        second = off >= s
        mid = blk * 2 * s + s
        mq = second[:, None] & (i[None, :] >= mid[:, None]) & (i[None, :] <= i[:, None])
        nk = (~second)[:, None] & (i[None, :] > i[:, None]) & (i[None, :] < mid[:, None])
        mats.append(mq | nk)
        masks.append(second[:, None] & (~second)[None, :] & (blk[:, None] == blk[None, :]))
    masks.append(i[:, None] == i[None, :])
    mall = np.concatenate(mats, 0).astype(np.float32)
    return jnp.asarray(mall, BF16), jnp.asarray(np.stack(masks).astype(np.float32)), len(levels)


def _ret_consts(c):
    n = LIN_TILE
    i = np.arange(n)
    log_g = np.log1p(-np.exp2(-5.0 - np.arange(H_A, dtype=np.float64)))
    same = (i[:, None] // c) == (i[None, :] // c)
    causal = same & (i[:, None] >= i[None, :])
    dmat = np.where(causal[None], np.exp(log_g[:, None, None] * (i[:, None] - i[None, :])[None]), 0.0)
    eq = np.exp(log_g[None, :] * ((i % c) + 1)[:, None])
    ek = np.exp(log_g[None, :] * (c - 1 - (i % c))[:, None])
    rep = lambda a: np.repeat(a, DK_A, axis=1)
    tabs = np.concatenate([rep(eq), rep(ek)], 1).astype(np.float32)
    glast = [float(np.exp(log_g[h] * c)) for h in range(H_A)]
    return jnp.asarray(dmat.astype(np.float32)), jnp.asarray(tabs), glast


def _split2(x):
    hi = _bf(x)
    return hi, _bf(x - hi.astype(F32))


def _lin_tile(q, k, v, gate, la, consts, state_get, state_put, *, gla, c, dk, dv, gnorm, glast):
    n_heads = q.shape[1] // dk
    n_sub = LIN_TILE // c
    if gla:
        mall, masks, nl = consts
        hi, lo = _split2(la)
        res = _dot(mall, jnp.concatenate([hi, lo], axis=1))
        res = res[:, 0:n_heads * dk] + res[:, n_heads * dk:]
        cum = res[0:LIN_TILE]
        qe = q * jnp.exp(cum)
        kk = k * jnp.exp(res[LIN_TILE:2 * LIN_TILE])
        el = [jnp.exp(res[(2 + l) * LIN_TILE:(3 + l) * LIN_TILE]) for l in range(nl)]
        qs = [q * e for e in el] + [q]
        ks = [k * e for e in el] + [k]
    else:
        dmat, tabs = consts
        qe = q * tabs[:, 0:n_heads * dk]
        kk = k * tabs[:, n_heads * dk:2 * n_heads * dk]
    v32 = v.astype(F32) if n_sub > 1 else None
    eye = None
    outs = []
    for h in range(n_heads):
        ksl = slice(h * dk, (h + 1) * dk)
        vsl = slice(h * dv, (h + 1) * dv)
        if gla:
            scores = masks[0] * _dot_nt(_bf(qs[0][:, ksl]), _bf(ks[0][:, ksl]))
            for l in range(1, nl + 1):
                scores = scores + masks[l] * _dot_nt(_bf(qs[l][:, ksl]), _bf(ks[l][:, ksl]))
        else:
            scores = dmat[h] * _dot_nt(_bf(q[:, ksl]), _bf(k[:, ksl]))
        o = _dot(_bf(scores), v[:, vsl])
        inter = []
        for u in range(n_sub):
            rs = slice(u * c, (u + 1) * c)
            s_old = state_get(u, h)
            inter.append(_dot(_bf(qe[rs, ksl]), _bf(s_old)))
            v_u = v[:, vsl] if n_sub == 1 else _bf(v32[rs, vsl])
            upd = _dot_tn(_bf(kk[rs, ksl]), v_u)
            if gla:
                if eye is None:
                    ii = lax.broadcasted_iota(jnp.int32, (dk, dk), 0)
                    jj = lax.broadcasted_iota(jnp.int32, (dk, dk), 1)
                    eye = (ii == jj).astype(F32)
                last = jnp.exp(cum[(u + 1) * c - 1:(u + 1) * c, ksl])
                dec = jnp.sum(eye * last, axis=1, keepdims=True)
            else:
                dec = glast[h]
            state_put(u, h, dec * s_old + upd)
        o = o + (inter[0] if n_sub == 1 else jnp.concatenate(inter, 0))
        o = _rms(o)
        if gnorm is not None:
            o = o * gnorm
        outs.append(o * _silu(gate[:, vsl]))
    return outs


def _lin_kernel(*refs, gla, c, seq, n_tiles, glast):
    it = iter(refs)
    q_ref, k_ref, v_ref, g_ref = next(it), next(it), next(it), next(it)
    la_ref = next(it) if gla else None
    gn_ref = next(it) if gla else None
    c1_ref, c2_ref = next(it), next(it)
    s_in_ref = None if seq else next(it)
    o_ref, s_out_ref = next(it), next(it)
    s_scr = next(it) if seq else None
    n_heads, dk, dv = H_D, DK_D, DV_D
    n_sub = LIN_TILE // c
    consts = (c1_ref[...], c2_ref[...], int(round(math.log2(c)))) if gla else (c1_ref[...], c2_ref[...])
    gnorm = gn_ref[...] if gla else None

    if seq:
        @pl.when(pl.program_id(1) == 0)
        def _():
            s_scr[...] = jnp.zeros_like(s_scr)

    for t in range(n_tiles):
        rows = slice(t * LIN_TILE, (t + 1) * LIN_TILE)
        if seq:
            get = lambda u, h: s_scr[h]

            def put(u, h, s):
                s_scr[h] = s
        else:
            get = lambda u, h, t=t: s_in_ref[t * n_sub + u, h]

            def put(u, h, s, t=t):
                s_out_ref[t * n_sub + u, h] = s
        outs = _lin_tile(q_ref[rows, :], k_ref[rows, :], v_ref[rows, :], g_ref[rows, :],
                         la_ref[rows, :] if gla else None, consts, get, put,
                         gla=gla, c=c, dk=dk, dv=dv, gnorm=gnorm, glast=glast)
        for h in range(n_heads):
            o_ref[rows, h * dv:(h + 1) * dv] = _bf(outs[h])

    if seq:
        @pl.when(pl.program_id(1) == pl.num_programs(1) - 1)
        def _():
            s_out_ref[0] = s_scr[...]


def _lin_attn(q, k, v, gate, la, gnorm, state, layer, *, gla, n_prompt, t_len, b_prompt, c_s):
    n = q.shape[0]
    n_s = n - n_prompt
    res = []
    for seq in (True, False):
        c = LIN_TILE if seq else c_s
        if gla:
            c1, c2, _ = _gla_consts(c)
            glast = None
        else:
            c1, c2, glast = _ret_consts(c)
        rows = _tile(t_len if seq else n_s, 256)
        n_tiles = rows // LIN_TILE
        full = lambda a: pl.BlockSpec(a.shape, lambda *_: (0,) * a.ndim)
        if seq:
            per = t_len // rows
            grid = (b_prompt, per)
            rix = lambda b, i: (b * per + i, 0)
            s_shape = (b_prompt, H_D, DK_D, DV_D)
            s_spec = pl.BlockSpec((1, H_D, DK_D, DV_D), lambda b, i: (b, 0, 0, 0))
            sem = ("parallel", "arbitrary")
            scratch = [pltpu.VMEM((H_D, DK_D, DV_D), F32)]
        else:
            off = n_prompt // rows
            grid = (n_s // rows,)
            rix = lambda i: (off + i, 0)
            nb = rows // c
            s_shape = state.shape[1:]
            s_spec = pl.BlockSpec((nb, H_D, DK_D, DV_D), lambda i: (i, 0, 0, 0))
            s_in_spec = pl.BlockSpec((None, nb, H_D, DK_D, DV_D), lambda i: (layer, i, 0, 0, 0))
            sem = ("parallel",)
            scratch = []
        row = lambda w: pl.BlockSpec((rows, w), rix)
        args = [q, k, v, gate] + ([la, gnorm] if gla else []) + [c1, c2] + ([] if seq else [state])
        specs = [row(256), row(256), row(512), row(512)] + ([row(256), full(gnorm)] if gla else []) + [full(c1), full(c2)]
        specs += [] if seq else [s_in_spec]
        n_rows = n_prompt if seq else n_s
        o_rix = (lambda b, i: (b * per + i, 0)) if seq else (lambda i: (i, 0))
        o, s_out = pl.pallas_call(
            functools.partial(_lin_kernel, gla=gla, c=c, seq=seq, n_tiles=n_tiles, glast=glast),
            grid=grid,
            in_specs=specs,
            out_specs=[pl.BlockSpec((rows, 512), o_rix), s_spec],
            out_shape=[jax.ShapeDtypeStruct((n_rows, 512), BF16), jax.ShapeDtypeStruct(s_shape, F32)],
            scratch_shapes=scratch,
            compiler_params=_params(*sem),
            name=("gla" if gla else "ret") + ("_prompt" if seq else "_sample"),
        )(*args)
        res.append((o, s_out))
    return res[0][0], res[1][0], res[0][1], res[1][1]


def _col_blocks(s):
    return [s[:, c * LANES:(c + 1) * LANES] for c in range(s.shape[1] // LANES)]


def _block_max(s):
    blocks = _col_blocks(s)
    mc = blocks[0]
    for b in blocks[1:]:
        mc = jnp.maximum(mc, b)
    return jnp.max(mc, axis=-1, keepdims=True)


def _flash_update(s, m_ref, acc_ref, idx, v1):
    m_old = m_ref[idx]
    m_new = jnp.maximum(m_old, _block_max(s))
    alpha = jnp.exp2(m_old - m_new)
    p = jnp.concatenate([_bf(jnp.exp2(b - m_new)) for b in _col_blocks(s)], axis=1)
    reps = acc_ref.shape[-1] // LANES
    acc_ref[idx] = jnp.concatenate([alpha] * reps, axis=1) * acc_ref[idx] + _dot(p, v1)
    m_ref[idx] = m_new


def _causal_bias(tq, tk):
    r = lax.broadcasted_iota(jnp.int32, (tq, tk), 0)
    c = lax.broadcasted_iota(jnp.int32, (tq, tk), 1)
    return jnp.where(r >= c, 0.0, NEG_BIG).astype(F32)


def _mla_prompt_kernel(q_ref, kv_ref, wvu_ref, o_ref, m_ref, l_ref, acc_ref):
    i, j = pl.program_id(1), pl.program_id(2)
    tq = q_ref.shape[0]
    qw = KV_LORA + LANES

    @pl.when(j == 0)
    def _():
        m_ref[...] = jnp.full_like(m_ref, NEG_BIG)
        l_ref[...] = jnp.zeros_like(l_ref)
        acc_ref[...] = jnp.zeros_like(acc_ref)

    def step(masked):
        kv = kv_ref[...]
        bias = _causal_bias(tq, tq) if masked else None
        for h in range(H_B):
            s = _dot_nt(q_ref[:, h * qw:(h + 1) * qw], kv)
            if masked:
                s = s + bias
            m_old = m_ref[h]
            m_new = jnp.maximum(m_old, _block_max(s))
            alpha = jnp.exp2(m_old - m_new)
            blocks = [jnp.exp2(b - m_new) for b in _col_blocks(s)]
            lsum = blocks[0]
            for b in blocks[1:]:
                lsum = lsum + b
            l_ref[h] = alpha * l_ref[h] + lsum
            p = jnp.concatenate([_bf(b) for b in blocks], axis=1)
            acc_ref[h] = jnp.concatenate([alpha] * (KV_LORA // LANES), axis=1) * acc_ref[h] + _dot(p, kv[:, 0:KV_LORA])
            m_ref[h] = m_new

    @pl.when(j < i)
    def _():
        step(False)

    @pl.when(j == i)
    def _():
        step(True)
        for h in range(H_B):
            lat = acc_ref[h] / jnp.sum(l_ref[h], axis=-1, keepdims=True)
            o_ref[:, h * V_B:(h + 1) * V_B] = _bf(_dot(_bf(lat), wvu_ref[h]))


def _mla_prompt(qm, kvm, w_vu, b_prompt, t_len, tq):
    n_q = t_len // tq
    qw = H_B * (KV_LORA + LANES)
    return pl.pallas_call(
        _mla_prompt_kernel,
        grid=(b_prompt, n_q, n_q),
        in_specs=[pl.BlockSpec((tq, qw), lambda b, i, j: (b * n_q + i, 0)),
                  pl.BlockSpec((tq, KV_LORA + LANES), lambda b, i, j: (b * n_q + jnp.minimum(i, j), 0)),
                  pl.BlockSpec(w_vu.shape, lambda b, i, j: (0, 0, 0))],
        out_specs=pl.BlockSpec((tq, H_B * V_B), lambda b, i, j: (b * n_q + i, 0)),
        out_shape=jax.ShapeDtypeStruct((b_prompt * t_len, H_B * V_B), BF16),
        scratch_shapes=[pltpu.VMEM((H_B, tq, LANES), F32), pltpu.VMEM((H_B, tq, LANES), F32),
                        pltpu.VMEM((H_B, tq, KV_LORA), F32)],
        compiler_params=_params("parallel", "parallel", "arbitrary"),
        name="mla_prompt",
    )(qm, kvm, w_vu)


def _lambda(lam_ref, lam_init):
    l = lam_ref[...]
    a = jnp.sum(l[0:1] * l[1:2], axis=-1, keepdims=True)
    b = jnp.sum(l[2:3] * l[3:4], axis=-1, keepdims=True)
    return jnp.exp(a) - jnp.exp(b) + lam_init


def _diff_prompt_kernel(q_ref, k_ref, v_ref, lam_ref, gsub_ref, o_ref, m_ref, acc_ref, *, lam_init, ij=None):
    i, j = (pl.program_id(1), pl.program_id(2)) if ij is None else ij
    tq = q_ref.shape[0]

    @pl.when(j == 0)
    def _():
        m_ref[...] = jnp.full_like(m_ref, NEG_BIG)
        acc_ref[...] = jnp.zeros_like(acc_ref)

    def step(masked):
        bias = _causal_bias(tq, tq) if masked else None
        ones = jnp.ones((tq, LANES), BF16)
        for g in range(KV_C):
            kg = k_ref[:, g * LANES:(g + 1) * LANES]
            v1 = jnp.concatenate([v_ref[:, g * LANES:(g + 1) * LANES], ones], axis=1)
            for rs in range(2 * REP_C):
                idx = g * 2 * REP_C + rs
                s = _dot_nt(q_ref[:, idx * LANES:(idx + 1) * LANES], kg)
                if masked:
                    s = s + bias
                _flash_update(s, m_ref, acc_ref, idx, v1)

    @pl.when(j < i)
    def _():
        step(False)

    @pl.when(j == i)
    def _():
        step(True)
        lam = _lambda(lam_ref, lam_init)
        for gr in range(KV_C * REP_C):
            a1, a2 = acc_ref[2 * gr], acc_ref[2 * gr + 1]
            o = a1[:, 0:LANES] / a1[:, LANES:2 * LANES] - lam * (a2[:, 0:LANES] / a2[:, LANES:2 * LANES])
            o_ref[:, gr * LANES:(gr + 1) * LANES] = _bf(_rms(o) * gsub_ref[...] * (1.0 - lam_init))


def _diff_prompt(qdm, kcm, vcm, lam4, g_sub, lam_init, b_prompt, t_len, tq):
    n_q = t_len // tq
    n_maps = KV_C * REP_C * 2
    return pl.pallas_call(
        functools.partial(_diff_prompt_kernel, lam_init=lam_init),
        grid=(b_prompt, n_q, n_q),
        in_specs=[pl.BlockSpec((tq, n_maps * LANES), lambda b, i, j: (b * n_q + i, 0)),
                  pl.BlockSpec((tq, KV_C * LANES), lambda b, i, j: (b * n_q + jnp.minimum(i, j), 0)),
                  pl.BlockSpec((tq, KV_C * LANES), lambda b, i, j: (b * n_q + jnp.minimum(i, j), 0)),
                  pl.BlockSpec(lam4.shape, lambda b, i, j: (0, 0)),
                  pl.BlockSpec(g_sub.shape, lambda b, i, j: (0, 0))],
        out_specs=pl.BlockSpec((tq, H_C * 2 * D_C), lambda b, i, j: (b * n_q + i, 0)),
        out_shape=jax.ShapeDtypeStruct((b_prompt * t_len, H_C * 2 * D_C), BF16),
        scratch_shapes=[pltpu.VMEM((n_maps, tq, LANES), F32), pltpu.VMEM((n_maps, tq, 2 * LANES), F32)],
        compiler_params=_params("parallel", "parallel", "arbitrary"),
        name="diff_prompt",
    )(qdm, kcm, vcm, lam4, g_sub)


def _page_copies(pt_ref, caches, bufs, sems, layer, step, slot, pg, n_groups):
    b = step // n_groups
    p0 = (step % n_groups) * pg
    out = []
    for p in range(pg):
        page = pt_ref[b, p0 + p]
        for cache, buf, sem in zip(caches, bufs, sems):
            out.append(pltpu.make_async_copy(cache.at[layer, page], buf.at[slot, p], sem.at[slot]))
    return out


def _prefetch_pages(pt_ref, caches, bufs, sems, layer, pg, n_groups, step=None, n_steps=None):
    if step is None:
        step = pl.program_id(0) * n_groups + pl.program_id(1)
        n_steps = pl.num_programs(0) * n_groups
    slot = step % 2

    @pl.when(step == 0)
    def _():
        for cp in _page_copies(pt_ref, caches, bufs, sems, layer, step, slot, pg, n_groups):
            cp.start()

    @pl.when(step + 1 < n_steps)
    def _():
        for cp in _page_copies(pt_ref, caches, bufs, sems, layer, step + 1, 1 - slot, pg, n_groups):
            cp.start()

    for cp in _page_copies(pt_ref, caches, bufs, sems, layer, step, slot, pg, n_groups):
        cp.wait()
    return slot


def _decode_update(s_parts, v_parts, tail, m_ref, l_ref, acc_ref):
    m_old = m_ref[...]
    mx = _block_max(s_parts[0])
    for sp in s_parts[1:]:
        mx = jnp.maximum(mx, _block_max(sp))
    if tail is not None:
        mx = jnp.maximum(mx, jnp.max(tail[0], axis=-1, keepdims=True))
    m_new = jnp.maximum(m_old, mx)
    alpha = jnp.exp2(m_old - m_new)
    lsum, pvs = None, [None, None]
    for idx, (sp, vp) in enumerate(zip(s_parts, v_parts)):
        blocks = [jnp.exp2(b - m_new) for b in _col_blocks(sp)]
        for b in blocks:
            lsum = b if lsum is None else lsum + b
        p = jnp.concatenate([_bf(b) for b in blocks], axis=1)
        if vp.shape[1] > LANES:
            d = jnp.concatenate([_dot(p, vp[:, c * LANES:(c + 1) * LANES]) for c in range(vp.shape[1] // LANES)], axis=1)
        else:
            d = _dot(p, vp)
        pvs[idx % 2] = d if pvs[idx % 2] is None else pvs[idx % 2] + d
    pv = pvs[0] if pvs[1] is None else pvs[0] + pvs[1]
    lrow = jnp.sum(lsum, axis=-1, keepdims=True)
    if tail is not None:
        pt = jnp.exp2(tail[0] - m_new[:, 0:1])
        lrow = lrow + jnp.sum(pt, axis=-1, keepdims=True)
        pv = pv + _dot(_bf(pt), tail[1])
    l_ref[...] = alpha * l_ref[...] + lrow
    acc_ref[...] = jnp.concatenate([alpha] * (acc_ref.shape[-1] // LANES), axis=1) * acc_ref[...] + pv
    m_ref[...] = m_new


def _mla_decode_kernel(pt_ref, q_ref, kvn_ref, wvu_ref, ckv_hbm, krt_hbm, o_ref,
                       ckv_buf, krt_buf, sem_c, sem_r, m_ref, l_ref, acc_ref, *, layer, pg, n_groups, pc):
    gi = pl.program_id(1)
    slot = _prefetch_pages(pt_ref, (ckv_hbm, krt_hbm), (ckv_buf, krt_buf), (sem_c, sem_r), layer, pg, n_groups)
    page = ckv_buf.shape[2]
    ts = q_ref.shape[0] // H_B

    @pl.when(gi == 0)
    def _():
        m_ref[...] = jnp.full_like(m_ref, NEG_BIG)
        l_ref[...] = jnp.zeros_like(l_ref)
        acc_ref[...] = jnp.zeros_like(acc_ref)

    def step(last):
        q = q_ref[...]
        q_lat, q_rope = q[:, 0:KV_LORA], q[:, KV_LORA:KV_LORA + ROPE_B]
        s_parts, v_parts = [], []
        for ch in range(pg // pc):
            kc = _bf(ckv_buf[slot, ch * pc:(ch + 1) * pc].reshape(pc * page, KV_LORA))
            krt = jnp.concatenate([_bf(krt_buf[slot, p]) for p in range(ch * pc, (ch + 1) * pc)], axis=1)
            s_parts.append(_dot_nt(q_lat, kc) + _dot(q_rope, krt))
            v_parts.append(kc)
        tail = None
        if last:
            kvn = kvn_ref[...]
            s = _dot_nt(q[:, 0:KV_LORA + ROPE_B], kvn[:, 0:KV_LORA + ROPE_B])
            r = lax.broadcasted_iota(jnp.int32, s.shape, 0) % ts
            c = lax.broadcasted_iota(jnp.int32, s.shape, 1)
            tail = (jnp.where(r >= c, s, NEG_BIG), kvn[:, 0:KV_LORA])
        _decode_update(s_parts, v_parts, tail, m_ref, l_ref, acc_ref)

    if n_groups > 1:
        @pl.when(gi < n_groups - 1)
        def _():
            step(False)

    @pl.when(gi == n_groups - 1)
    def _():
        step(True)
        lat = acc_ref[...] / l_ref[:, 0:1]
        for h in range(H_B):
            o_ref[:, h * V_B:(h + 1) * V_B] = _bf(_dot(_bf(lat[h * ts:(h + 1) * ts]), wvu_ref[h]))


def _mla_decode(page_table, q_s, kv_new, w_vu, cache_ckv, cache_kr, layer, pg, pc):
    bs, n_pages = page_table.shape
    rows = q_s.shape[1]
    ts = rows // H_B
    page = cache_ckv.shape[2]
    n_groups = n_pages // pg
    kern = functools.partial(_mla_decode_kernel, layer=layer, pg=pg, n_groups=n_groups, pc=pc)
    return pl.pallas_call(
        kern,
        grid_spec=pltpu.PrefetchScalarGridSpec(
            num_scalar_prefetch=1,
            grid=(bs, n_groups),
            in_specs=[pl.BlockSpec((None, rows, KV_LORA + LANES), lambda b, g, pt: (b, 0, 0)),
                      pl.BlockSpec((None, ts, KV_LORA + LANES), lambda b, g, pt: (b, 0, 0)),
                      pl.BlockSpec(w_vu.shape, lambda b, g, pt: (0, 0, 0)),
                      pl.BlockSpec(memory_space=pl.ANY), pl.BlockSpec(memory_space=pl.ANY)],
            out_specs=pl.BlockSpec((None, ts, H_B * V_B), lambda b, g, pt: (b, 0, 0)),
            scratch_shapes=[pltpu.VMEM((2, pg, page, KV_LORA), F32), pltpu.VMEM((2, pg, ROPE_B, page), F32),
                            pltpu.SemaphoreType.DMA((2,)), pltpu.SemaphoreType.DMA((2,)),
                            pltpu.VMEM((rows, LANES), F32), pltpu.VMEM((rows, LANES), F32),
                            pltpu.VMEM((rows, KV_LORA), F32)]),
        out_shape=jax.ShapeDtypeStruct((bs, ts, H_B * V_B), BF16),
        compiler_params=_params("arbitrary", "arbitrary"),
        name="mla_decode",
    )(page_table, q_s, kv_new, w_vu, cache_ckv, cache_kr)


def _diff_decode_kernel(pt_ref, q_ref, kn_ref, vn_ref, lam_ref, gsub_ref, k_hbm, v_hbm, o_ref,
                        k_buf, v_buf, sem_k, sem_v, m_ref, l_ref, acc_ref, *, layer, pg, n_groups, pu, pc, lam_init,
                        steps=None):
    gi = pl.program_id(1) if steps is None else 0
    slot = _prefetch_pages(pt_ref, (k_hbm, v_hbm), (k_buf, v_buf), (sem_k, sem_v), layer, pg, n_groups,
                           *(steps or ()))
    prow = k_buf.shape[2]
    rows = q_ref.shape[0]
    ts = rows // (KV_C * REP_C * 2)

    @pl.when(gi == 0)
    def _():
        m_ref[...] = jnp.full_like(m_ref, NEG_BIG)
        l_ref[...] = jnp.zeros_like(l_ref)
        acc_ref[...] = jnp.zeros_like(acc_ref)

    def group_match(n_cols):
        rg = lax.broadcasted_iota(jnp.int32, (rows, n_cols), 0) // (rows // KV_C)
        cg = lax.broadcasted_iota(jnp.int32, (rows, n_cols), 1) % KV_C
        return rg == cg

    def step(last):
        q = q_ref[...]
        same = group_match(pc * prow)
        for u in range(pg // pu):
            s_parts, v_parts = [], []
            for ch in range(u * (pu // pc), (u + 1) * (pu // pc)):
                kk = _bf(k_buf[slot, ch * pc:(ch + 1) * pc].reshape(pc * prow, LANES))
                s_parts.append(jnp.where(same, _dot_nt(q, kk), NEG_BIG))
                v_parts.append(_bf(v_buf[slot, ch * pc:(ch + 1) * pc].reshape(pc * prow, LANES)))
            tail = None
            if last and u == pg // pu - 1:
                s = _dot_nt(q, kn_ref[...])
                r = lax.broadcasted_iota(jnp.int32, s.shape, 0) % ts
                c = lax.broadcasted_iota(jnp.int32, s.shape, 1) // KV_C
                tail = (jnp.where(group_match(ts * KV_C) & (r >= c), s, NEG_BIG), vn_ref[...])
            _decode_update(s_parts, v_parts, tail, m_ref, l_ref, acc_ref)

    if n_groups > 1:
        @pl.when(gi < n_groups - 1)
        def _():
            step(False)

    @pl.when(gi == n_groups - 1)
    def _():
        step(True)
        o = acc_ref[...] / l_ref[:, 0:1]
        lam = _lambda(lam_ref, lam_init)
        for gr in range(KV_C * REP_C):
            o1 = o[(2 * gr) * ts:(2 * gr + 1) * ts]
            o2 = o[(2 * gr + 1) * ts:(2 * gr + 2) * ts]
            o_ref[:, gr * LANES:(gr + 1) * LANES] = _bf(_rms(o1 - lam * o2) * gsub_ref[...] * (1.0 - lam_init))


def _diff_decode(page_table, q_s, k_new, v_new, lam4, g_sub, cache_k, cache_v, layer, lam_init, pg, pu, pc):
    bs, n_pages = page_table.shape
    rows = q_s.shape[1]
    ts = rows // (KV_C * REP_C * 2)
    prow = cache_k.shape[2]
    n_groups = n_pages // pg
    kern = functools.partial(_diff_decode_kernel, layer=layer, pg=pg, n_groups=n_groups, pu=pu, pc=pc,
                             lam_init=lam_init)
    return pl.pallas_call(
        kern,
        grid_spec=pltpu.PrefetchScalarGridSpec(
            num_scalar_prefetch=1,
            grid=(bs, n_groups),
            in_specs=[pl.BlockSpec((None, rows, LANES), lambda b, g, pt: (b, 0, 0)),
                      pl.BlockSpec((None, ts * KV_C, LANES), lambda b, g, pt: (b, 0, 0)),
                      pl.BlockSpec((None, ts * KV_C, LANES), lambda b, g, pt: (b, 0, 0)),
                      pl.BlockSpec(lam4.shape, lambda b, g, pt: (0, 0)),
                      pl.BlockSpec(g_sub.shape, lambda b, g, pt: (0, 0)),
                      pl.BlockSpec(memory_space=pl.ANY), pl.BlockSpec(memory_space=pl.ANY)],
            out_specs=pl.BlockSpec((None, ts, H_C * 2 * D_C), lambda b, g, pt: (b, 0, 0)),
            scratch_shapes=[pltpu.VMEM((2, pg, prow, LANES), F32), pltpu.VMEM((2, pg, prow, LANES), F32),
                            pltpu.SemaphoreType.DMA((2,)), pltpu.SemaphoreType.DMA((2,)),
                            pltpu.VMEM((rows, LANES), F32), pltpu.VMEM((rows, LANES), F32),
                            pltpu.VMEM((rows, LANES), F32)]),
        out_shape=jax.ShapeDtypeStruct((bs, ts, H_C * 2 * D_C), BF16),
        compiler_params=_params("arbitrary", "arbitrary"),
        name="diff_decode",
    )(page_table, q_s, k_new, v_new, lam4, g_sub, cache_k, cache_v)


def _diff_fused_kernel(pt_ref, sc_ref, qd_ref, kn_ref, vn_ref, lam_ref, gsub_ref, qp_ref, kp_ref, vp_ref,
                       k_hbm, v_hbm, od_ref, op_ref, k_buf, v_buf, sem_k, sem_v, m_ref, l_ref, acc_ref,
                       pm_ref, pacc_ref, *, layer, pg, pu, pc, lam_init):
    s = pl.program_id(0)
    _diff_decode_kernel(pt_ref, qd_ref, kn_ref, vn_ref, lam_ref, gsub_ref, k_hbm, v_hbm, od_ref,
                        k_buf, v_buf, sem_k, sem_v, m_ref, l_ref, acc_ref, layer=layer, pg=pg, n_groups=1,
                        pu=pu, pc=pc, lam_init=lam_init, steps=(s, pl.num_programs(0)))

    @pl.when(sc_ref[3, s] == 1)
    def _():
        _diff_prompt_kernel(qp_ref, kp_ref, vp_ref, lam_ref, gsub_ref, op_ref, pm_ref, pacc_ref,
                            lam_init=lam_init, ij=(sc_ref[1, s], sc_ref[2, s]))


def _tile_schedule(n_steps, b_prompt, n_q):
    pairs = [(b, i, j) for b in range(b_prompt) for i in range(n_q) for j in range(i + 1)]
    sched = np.zeros((4, n_steps), np.int32)
    at = {(k * n_steps) // len(pairs): k for k in range(len(pairs))}
    cur = pairs[0]
    for s in range(n_steps):
        if s in at:
            cur = pairs[at[s]]
            sched[3, s] = 1
        sched[0:3, s] = cur
    return jnp.asarray(sched)


def _diff_fused(page_table, q_s, k_new, v_new, lam4, g_sub, cache_k, cache_v, qdm, kcm, vcm, layer, lam_init,
                pg, pu, pc, b_prompt, t_len, tq):
    bs, n_pages = page_table.shape
    assert pg == n_pages
    rows = q_s.shape[1]
    ts = rows // (KV_C * REP_C * 2)
    prow = cache_k.shape[2]
    n_q = t_len // tq
    n_maps = KV_C * REP_C * 2
    sched = _tile_schedule(bs, b_prompt, n_q)
    dec = lambda s, pt, sc: (s, 0, 0)
    const = lambda s, pt, sc: (0, 0)
    q_ix = lambda s, pt, sc: (sc[0, s] * n_q + sc[1, s], 0)
    kv_ix = lambda s, pt, sc: (sc[0, s] * n_q + sc[2, s], 0)
    kern = functools.partial(_diff_fused_kernel, layer=layer, pg=pg, pu=pu, pc=pc, lam_init=lam_init)
    return pl.pallas_call(
        kern,
        grid_spec=pltpu.PrefetchScalarGridSpec(
            num_scalar_prefetch=2,
            grid=(bs,),
            in_specs=[pl.BlockSpec((None, rows, LANES), dec),
                      pl.BlockSpec((None, ts * KV_C, LANES), dec),
                      pl.BlockSpec((None, ts * KV_C, LANES), dec),
                      pl.BlockSpec(lam4.shape, const), pl.BlockSpec(g_sub.shape, const),
                      pl.BlockSpec((tq, n_maps * LANES), q_ix),
                      pl.BlockSpec((tq, KV_C * LANES), kv_ix), pl.BlockSpec((tq, KV_C * LANES), kv_ix),
                      pl.BlockSpec(memory_space=pl.ANY), pl.BlockSpec(memory_space=pl.ANY)],
            out_specs=[pl.BlockSpec((None, ts, H_C * 2 * D_C), dec),
                       pl.BlockSpec((tq, H_C * 2 * D_C), q_ix)],
            scratch_shapes=[pltpu.VMEM((2, pg, prow, LANES), F32), pltpu.VMEM((2, pg, prow, LANES), F32),
                            pltpu.SemaphoreType.DMA((2,)), pltpu.SemaphoreType.DMA((2,)),
                            pltpu.VMEM((rows, LANES), F32), pltpu.VMEM((rows, LANES), F32),
                            pltpu.VMEM((rows, LANES), F32),
                            pltpu.VMEM((n_maps, tq, LANES), F32), pltpu.VMEM((n_maps, tq, 2 * LANES), F32)]),
        out_shape=[jax.ShapeDtypeStruct((bs, ts, H_C * 2 * D_C), BF16),
                   jax.ShapeDtypeStruct((b_prompt * t_len, H_C * 2 * D_C), BF16)],
        compiler_params=pltpu.CompilerParams(dimension_semantics=("arbitrary",), vmem_limit_bytes=FUSED_VMEM_LIMIT),
        name="diff_fused",
    )(page_table, sched, q_s, k_new, v_new, lam4, g_sub, qdm, kcm, vcm, cache_k, cache_v)


def _layernorm(z, g, b):
    mu = jnp.mean(z, axis=-1, keepdims=True)
    d = z - mu
    var = jnp.mean(d * d, axis=-1, keepdims=True)
    return d * lax.rsqrt(var + EPS) * g + b


def _split_rows(tm, n_pt):
    return (lambda i, *_: (jnp.minimum(i, n_pt - 1), 0)), (lambda i, *_: (jnp.maximum(i - n_pt, 0), 0))


def _out_proj_kernel(o1p_ref, o1s_ref, o2p_ref, o2s_ref, w1_ref, w2_ref, x_ref, g_ref, b_ref, y_ref, *, n_pt):
    def body(o1_ref, o2_ref):
        y = _dot(o1_ref[...], w1_ref[...]) + _dot(o2_ref[...], w2_ref[...])
        y_ref[...] = _layernorm(ALPHA * x_ref[...] + y, g_ref[...], b_ref[...])

    @pl.when(pl.program_id(0) < n_pt)
    def _():
        body(o1p_ref, o2p_ref)

    @pl.when(pl.program_id(0) >= n_pt)
    def _():
        body(o1s_ref, o2s_ref)


def _out_proj(o1p, o1s, o2p, o2s, w_out, layer, x, g, b, tm):
    n = x.shape[0]
    half = o1p.shape[1]
    n_pt = o1p.shape[0] // tm
    pix, six = _split_rows(tm, n_pt)
    row = lambda w: pl.BlockSpec((tm, w), lambda i: (i, 0))
    vec = pl.BlockSpec((1, D_MODEL), lambda i: (0, 0))
    return pl.pallas_call(
        functools.partial(_out_proj_kernel, n_pt=n_pt),
        grid=(n // tm,),
        in_specs=[pl.BlockSpec((tm, half), pix), pl.BlockSpec((tm, half), six),
                  pl.BlockSpec((tm, half), pix), pl.BlockSpec((tm, half), six),
                  pl.BlockSpec((None, half, D_MODEL), lambda i: (layer, 0, 0)),
                  pl.BlockSpec((None, half, D_MODEL), lambda i: (layer, 1, 0)), row(D_MODEL), vec, vec],
        out_specs=row(D_MODEL),
        out_shape=jax.ShapeDtypeStruct((n, D_MODEL), F32),
        compiler_params=_params("arbitrary"),
        name="out_proj_ln",
    )(o1p, o1s, o2p, o2s, w_out, w_out, x, g, b)


def _mlp_kernel(x_ref, w1_ref, w2_ref, g_ref, b_ref, *outs, n_pt, n_sub):
    sub = x_ref.shape[0] // n_sub
    ys = []
    for u in range(n_sub):
        x = x_ref[u * sub:(u + 1) * sub, :]
        h = jnp.maximum(_dot(_bf(x), w1_ref[...]), 0.0)
        y = _dot(_bf(h * h), w2_ref[...])
        ys.append(_layernorm(ALPHA * x + y, g_ref[...], b_ref[...]))
    y = ys[0] if n_sub == 1 else jnp.concatenate(ys, axis=0)
    if n_pt is None:
        outs[0][...] = y
    else:
        _store_split(n_pt, outs[0], outs[1], y)


def _mlp(x, w1, w2, layer, g, b, tm, n_prompt=None):
    n = x.shape[0]
    vec = pl.BlockSpec((1, D_MODEL), lambda i: (0, 0))
    if n_prompt is None:
        n_pt = None
        out_specs = pl.BlockSpec((tm, D_MODEL), lambda i: (i, 0))
        out_shape = jax.ShapeDtypeStruct((n, D_MODEL), F32)
    else:
        n_pt = n_prompt // tm
        pix, six = _split_rows(tm, n_pt)
        out_specs = [pl.BlockSpec((tm, D_MODEL), pix), pl.BlockSpec((tm, D_MODEL), six)]
        out_shape = [jax.ShapeDtypeStruct((n_prompt, D_MODEL), F32), jax.ShapeDtypeStruct((n - n_prompt, D_MODEL), F32)]
    once = pl.Buffered(1)
    return pl.pallas_call(
        functools.partial(_mlp_kernel, n_pt=n_pt, n_sub=2 if tm % 16 == 0 else 1),
        grid=(n // tm,),
        in_specs=[pl.BlockSpec((tm, D_MODEL), lambda i: (i, 0)),
                  pl.BlockSpec((None, D_MODEL, D_FF), lambda i: (layer, 0, 0), pipeline_mode=once),
                  pl.BlockSpec((None, D_FF, D_MODEL), lambda i: (layer, 0, 0), pipeline_mode=once), vec, vec],
        out_specs=out_specs,
        out_shape=out_shape,
        compiler_params=_params("arbitrary"),
        name="mlp_ln",
    )(x, w1, w2, g, b)


def _pad_cols(w, to):
    return jnp.pad(w, ((0, 0), (0, to - w.shape[1])))


def _prep_even(w_in, w_qu, w_ku):
    w_in_p = _bf(_pad_cols(w_in, 2304))
    nope = w_qu[:, :, :NOPE_B].reshape(Q_LORA, H_B * NOPE_B)
    rope = jnp.pad(w_qu[:, :, NOPE_B:], ((0, 0), (0, 0), (0, LANES - ROPE_B))).reshape(Q_LORA, H_B * LANES)
    w_qu_p = _bf(jnp.concatenate([nope, rope], -1))
    w_ku_t = _bf(jnp.transpose(w_ku, (1, 2, 0)))
    return w_in_p, w_qu_p, w_ku_t


def _prep_odd(w_in, w_gu):
    return _bf(_pad_cols(w_in, 2688)), _bf(jnp.pad(w_gu, ((0, LANES - GK_RANK), (0, 0))))


def kernel(x_prompt, x_sample, state_ret, cache_mla_ckv, cache_mla_krope, cache_diff_k, cache_diff_v, state_gla, page_table, w_in_even, g_q_lora, g_kv_lora, w_q_up, w_k_up, w_v_up, w_out_even, w_in_odd, lam_q1, lam_k1, lam_q2, lam_k2, g_subln, w_gate_up, b_gate_up, g_gla_norm, w_out_odd, ln1_g, ln1_b, ln2_g, ln2_b, w_ff1, w_ff2):
    bp, tp, _ = x_prompt.shape
    bs, ts, _ = x_sample.shape
    n_p, n_s = bp * tp, bs * ts
    n_pages, page = page_table.shape[1], cache_mla_ckv.shape[2]
    past_len = n_pages * page
    assert LIN_TILE % ts == 0 and n_s % LIN_TILE == 0 and tp % LIN_TILE == 0

    tm = _tile(math.gcd(tp, n_s), 512)
    tq = _tile(tp, TQ_PREF)
    pg = _tile(n_pages, 64)
    pg_mla = pg
    pu = _tile(pg, 32)
    pc = _tile(pu, 8)

    pos = np.concatenate([np.arange(tp), past_len + np.tile(np.arange(ts), bs)])
    tab_ret = _rope_table(pos, RET_THETA, DK_A, DK_A, 2)
    tab_mla = _rope_table(pos, ROPE_THETA, ROPE_B, ROPE_B, 1)
    tab_diff = _rope_table(pos, ROPE_THETA, ROT_C, D_C, 2)

    cache_krt = jnp.swapaxes(cache_mla_krope, 2, 3)
    cache_k = cache_diff_k.reshape(cache_diff_k.shape[:2] + (page * KV_C, 2 * D_C))
    cache_v = cache_diff_v.reshape(cache_diff_v.shape[:2] + (page * KV_C, 2 * D_C))

    x = jnp.concatenate([x_prompt.reshape(n_p, D_MODEL), x_sample.reshape(n_s, D_MODEL)], 0)
    row2 = lambda v: v.reshape(1, -1)
    w_ff1_b, w_ff2_b, w_out_even_b, w_out_odd_b = _bf(w_ff1), _bf(w_ff2), _bf(w_out_even), _bf(w_out_odd)
    outs = {k: [] for k in ("p_ret", "p_ckv", "p_kr", "p_dk", "p_dv", "p_gla", "s_ret", "s_ckv", "s_kr", "s_dk", "s_dv", "s_gla")}
    for l in range(DEPTH):
        j = l // 2
        if l % 2 == 0:
            w_in_p, w_qu_p, w_ku_t = _prep_even(w_in_even[j], w_q_up[j], w_k_up[j])
            w_vu = _bf(jnp.transpose(w_v_up[j], (1, 0, 2)))
            qa, ka, va, ga, qm, kvm, ckv_p, ckv_s, kr_p, kr_s = _proj_even(
                x, w_in_p, tab_ret, tab_mla, row2(g_q_lora[j]), row2(g_kv_lora[j]), w_qu_p, w_ku_t, tp, n_p, tm)
            o_ap, o_as, st_p, st_s = _lin_attn(qa, ka, va, ga, None, None, state_ret, j, gla=False, n_prompt=n_p,
                                               t_len=tp, b_prompt=bp, c_s=ts)
            o_bp = _mla_prompt(qm, kvm, w_vu, bp, tp, tq)
            qw = KV_LORA + LANES
            q_s = qm[n_p:].reshape(bs, ts, H_B, qw).transpose(0, 2, 1, 3).reshape(bs, H_B * ts, qw)
            o_bs = _mla_decode(page_table, q_s, kvm[n_p:].reshape(bs, ts, qw), w_vu, cache_mla_ckv, cache_krt,
                               j, pg_mla, pc)
            x = _out_proj(o_ap, o_as, o_bp, o_bs.reshape(n_s, H_B * V_B), w_out_even_b, j, x, row2(ln1_g[l]),
                          row2(ln1_b[l]), tm)
            outs["p_ret"].append(st_p)
            outs["s_ret"].append(st_s)
            outs["p_ckv"].append(ckv_p.reshape(bp, tp, KV_LORA))
            outs["s_ckv"].append(ckv_s.reshape(bs, ts, KV_LORA))
            outs["p_kr"].append(kr_p.reshape(bp, tp, ROPE_B))
            outs["s_kr"].append(kr_s.reshape(bs, ts, ROPE_B))
        else:
            lam_init = 0.8 - 0.6 * math.exp(-0.3 * l)
            w_in_p, w_gu_p = _prep_odd(w_in_odd[j], w_gate_up[j])
            lam4 = jnp.stack([lam_q1[j], lam_k1[j], lam_q2[j], lam_k2[j]])
            g_sub = row2(g_subln[j])
            qdm, kcm, vcm, qd, kd, vd, gd, la, kc_p, kc_s, vc_p, vc_s = _proj_odd(
                x, w_in_p, tab_diff, w_gu_p, row2(b_gate_up[j]), tp, n_p, tm)
            o_dp, o_ds, st_p, st_s = _lin_attn(qd, kd, vd, gd, la, row2(g_gla_norm[j]), state_gla, j, gla=True,
                                               n_prompt=n_p, t_len=tp, b_prompt=bp, c_s=ts)
            n_maps = KV_C * REP_C * 2
            q_s = qdm[n_p:].reshape(bs, ts, n_maps, LANES).transpose(0, 2, 1, 3).reshape(bs, n_maps * ts, LANES)
            k_new = kcm[n_p:].reshape(bs, ts * KV_C, LANES)
            v_new = vcm[n_p:].reshape(bs, ts * KV_C, LANES)
            n_q = tp // tq
            if pg == n_pages and bp * n_q * (n_q + 1) // 2 <= bs:
                o_cs, o_cp = _diff_fused(page_table, q_s, k_new, v_new, lam4, g_sub, cache_k, cache_v, qdm, kcm, vcm,
                                         j, lam_init, pg, pu, pc, bp, tp, tq)
            else:
                o_cp = _diff_prompt(qdm, kcm, vcm, lam4, g_sub, lam_init, bp, tp, tq)
                o_cs = _diff_decode(page_table, q_s, k_new, v_new, lam4, g_sub, cache_k, cache_v, j, lam_init, pg, pu,
                                    pc)
            x = _out_proj(o_cp, o_cs.reshape(n_s, H_C * 2 * D_C), o_dp, o_ds, w_out_odd_b, j, x, row2(ln1_g[l]),
                          row2(ln1_b[l]), tm)
            outs["p_dk"].append(kc_p.reshape(bp, tp, KV_C, 2 * D_C))
            outs["s_dk"].append(kc_s.reshape(bs, ts, KV_C, 2 * D_C))
            outs["p_dv"].append(vc_p.reshape(bp, tp, KV_C, 2 * D_C))
            outs["s_dv"].append(vc_s.reshape(bs, ts, KV_C, 2 * D_C))
            outs["p_gla"].append(st_p)
            outs["s_gla"].append(st_s)
        x = _mlp(x, w_ff1_b, w_ff2_b, l, row2(ln2_g[l]), row2(ln2_b[l]), tm,
                 n_prompt=n_p if l == DEPTH - 1 else None)
    st = lambda k: jnp.stack(outs[k])
    return (x[0].reshape(bp, tp, D_MODEL), x[1].reshape(bs, ts, D_MODEL),
            st("p_ret"), st("p_ckv"), st("p_kr"), st("p_dk"), st("p_dv"), st("p_gla"),
            st("s_ret"), st("s_ckv"), st("s_kr"), st("s_dk"), st("s_dv"), st("s_gla"))
```

```python
import functools
import math

import numpy as np
import jax
import jax.numpy as jnp
from jax import lax
from jax.experimental import pallas as pl
from jax.experimental.pallas import tpu as pltpu

F32 = jnp.float32
BF16 = jnp.bfloat16

D_MODEL = 1024
DEPTH = 4
H_A, DK_A, DV_A = 4, 64, 128
RET_THETA = 10000.0
H_B, Q_LORA, KV_LORA, NOPE_B, ROPE_B, V_B = 4, 384, 256, 128, 64, 128
H_C, KV_C, D_C = 4, 2, 64
REP_C = H_C // KV_C
ROT_C = D_C // 4
H_D, DK_D, DV_D = 4, 64, 128
GK_RANK = 16
GATE_NORM = 16.0
D_FF = 4 * D_MODEL
ROPE_THETA = 500000.0
ALPHA = (2 * DEPTH) ** 0.25
EPS = 1e-5
EVEN_SPLITS = (H_A * DK_A, H_A * DK_A, H_A * DV_A, H_A * DV_A, Q_LORA, KV_LORA, ROPE_B)
ODD_SPLITS = (H_C * 2 * D_C, KV_C * 2 * D_C, KV_C * 2 * D_C, H_D * DK_D, H_D * DK_D, H_D * DV_D, H_D * DV_D, GK_RANK)

LANES = 128
LIN_TILE = 64
NEG_BIG = -1e30
TQ_PREF = 512
LOG2E = math.log2(math.e)
QSCALE_MLA = (NOPE_B + ROPE_B) ** -0.5 * LOG2E
QSCALE_DIFF = D_C ** -0.5 * LOG2E
VMEM_LIMIT = 52 * 1024 * 1024
FUSED_VMEM_LIMIT = 58 * 1024 * 1024


def _dot(a, b):
    return jnp.dot(a, b, preferred_element_type=F32)


def _dot_nt(a, b):
    return lax.dot_general(a, b, (((1,), (1,)), ((), ())), preferred_element_type=F32)


def _dot_tn(a, b):
    return lax.dot_general(a, b, (((0,), (0,)), ((), ())), preferred_element_type=F32)


def _bf(x):
    return x.astype(BF16)


def _rms(x):
    return x * lax.rsqrt(jnp.mean(x * x, axis=-1, keepdims=True) + EPS)


def _silu(x):
    return x / (1.0 + jnp.exp(-x))


def _tile(n, pref):
    t = min(n, pref)
    while n % t:
        t -= 8
    return t


def _params(*sem):
    return pltpu.CompilerParams(dimension_semantics=sem, vmem_limit_bytes=VMEM_LIMIT)


def _rope_table(pos, theta, rot, group, n_groups):
    half = rot // 2
    inv = theta ** (-np.arange(half, dtype=np.float64) / half)
    ang = pos.astype(np.float64)[:, None] * inv[None, :]
    cos, sin = np.cos(ang), np.sin(ang)
    n = pos.shape[0]
    one = np.ones((n, group - rot))
    zh = np.zeros((n, half))
    zr = np.zeros((n, group - rot))
    c = np.concatenate([cos, cos, one], -1)
    s1 = np.concatenate([-sin, zh, zr], -1)
    s2 = np.concatenate([zh, sin, zr], -1)
    pad = np.zeros((n, LANES - n_groups * group))
    cat = lambda t: np.concatenate([t] * n_groups + [pad], -1)
    return jnp.asarray(np.concatenate([cat(c), cat(s1), cat(s2)], -1), F32)


def _rope_slab(x, tab, half):
    c, s1, s2 = tab[:, 0:LANES], tab[:, LANES:2 * LANES], tab[:, 2 * LANES:3 * LANES]
    return x * c + pltpu.roll(x, LANES - half, 1) * s1 + pltpu.roll(x, half, 1) * s2


def _store_split(n_pt, p_ref, s_ref, val):
    @pl.when(pl.program_id(0) < n_pt)
    def _():
        p_ref[...] = val

    @pl.when(pl.program_id(0) >= n_pt)
    def _():
        s_ref[...] = val


def _proj_even_kernel(x_ref, w_ref, taba_ref, tabb_ref, gq_ref, gkv_ref, wqu_ref, wku_ref,
                      qa_ref, ka_ref, va_ref, ga_ref, qm_ref, kvm_ref, ckvp_ref, ckvs_ref, krp_ref, krs_ref, *, n_pt):
    h = _dot(_bf(x_ref[...]), w_ref[...])
    taba = taba_ref[...]
    tabb = tabb_ref[...]
    for j in range(2):
        sl = slice(j * LANES, (j + 1) * LANES)
        qa_ref[:, sl] = _rope_slab(h[:, j * LANES:(j + 1) * LANES], taba, DK_A // 2)
        ka_ref[:, sl] = _rope_slab(h[:, 256 + j * LANES:256 + (j + 1) * LANES], taba, DK_A // 2) * (DK_A ** -0.5)
    va_ref[...] = _bf(h[:, 512:1024])
    ga_ref[...] = h[:, 1024:1536]
    cqn = _rms(h[:, 1536:1920]) * gq_ref[...]
    ckvn = _rms(h[:, 1920:2176]) * gkv_ref[...]
    kr = _rope_slab(h[:, 2176:2304], tabb, ROPE_B // 2)
    _store_split(n_pt, ckvp_ref, ckvs_ref, ckvn)
    _store_split(n_pt, krp_ref, krs_ref, kr[:, 0:ROPE_B])
    kvm_ref[:, 0:KV_LORA] = _bf(ckvn)
    kvm_ref[:, KV_LORA:KV_LORA + LANES] = _bf(kr)
    q = _dot(_bf(cqn), wqu_ref[...])
    for hd in range(H_B):
        q_lat = _dot(_bf(q[:, hd * NOPE_B:(hd + 1) * NOPE_B]), wku_ref[hd])
        base = hd * (KV_LORA + LANES)
        qm_ref[:, base:base + KV_LORA] = _bf(q_lat * QSCALE_MLA)
        qr = _rope_slab(q[:, H_B * NOPE_B + hd * LANES:H_B * NOPE_B + (hd + 1) * LANES], tabb, ROPE_B // 2)
        qm_ref[:, base + KV_LORA:base + KV_LORA + LANES] = _bf(qr * QSCALE_MLA)


def _tab_index(tm, t_len, n_prompt):
    per = t_len // tm
    n_pt = n_prompt // tm
    return lambda i: (jnp.where(i < n_pt, i % per, per + i - n_pt), 0)


def _proj_even(x, w_in, taba, tabb, g_q, g_kv, w_qu, w_ku, t_len, n_prompt, tm):
    n = x.shape[0]
    row = lambda w: pl.BlockSpec((tm, w), lambda i: (i, 0))
    full = lambda a: pl.BlockSpec(a.shape, lambda i: (0,) * a.ndim)
    tix = _tab_index(tm, t_len, n_prompt)
    qw = H_B * (KV_LORA + LANES)
    n_pt = n_prompt // tm
    pix, six = _split_rows(tm, n_pt)
    outs = [(256, F32), (256, F32), (512, BF16), (512, F32), (qw, BF16), (KV_LORA + LANES, BF16)]
    split = [KV_LORA, ROPE_B]
    return pl.pallas_call(
        functools.partial(_proj_even_kernel, n_pt=n_pt),
        grid=(n // tm,),
        in_specs=[row(D_MODEL), full(w_in), pl.BlockSpec((tm, 3 * LANES), tix), pl.BlockSpec((tm, 3 * LANES), tix),
                  full(g_q), full(g_kv), full(w_qu), full(w_ku)],
        out_specs=[row(w) for w, _ in outs] + [pl.BlockSpec((tm, w), ix) for w in split for ix in (pix, six)],
        out_shape=[jax.ShapeDtypeStruct((n, w), d) for w, d in outs]
        + [jax.ShapeDtypeStruct((r, w), F32) for w in split for r in (n_prompt, n - n_prompt)],
        compiler_params=_params("arbitrary"),
        name="proj_even",
    )(x, w_in, taba, tabb, g_q, g_kv, w_qu, w_ku)


def _proj_odd_kernel(x_ref, w_ref, tabc_ref, wgu_ref, bgu_ref,
                     qdm_ref, kcm_ref, vcm_ref, qd_ref, kd_ref, vd_ref, gd_ref, la_ref,
                     kcp_ref, kcs_ref, vcp_ref, vcs_ref, *, n_pt):
    h = _dot(_bf(x_ref[...]), w_ref[...])
    tabc = tabc_ref[...]
    lane = lax.broadcasted_iota(jnp.int32, (1, LANES), 1)
    lo = (lane < D_C).astype(F32)
    hi = 1.0 - lo
    for j in range(4):
        qs = _rope_slab(h[:, j * LANES:(j + 1) * LANES], tabc, ROT_C // 2) * QSCALE_DIFF
        qdm_ref[:, (2 * j) * LANES:(2 * j + 1) * LANES] = _bf(qs * lo)
        qdm_ref[:, (2 * j + 1) * LANES:(2 * j + 2) * LANES] = _bf(qs * hi)
    kc = jnp.concatenate([_rope_slab(h[:, 512 + j * LANES:512 + (j + 1) * LANES], tabc, ROT_C // 2)
                          for j in range(KV_C)], axis=1)
    vc = h[:, 768:1024]
    _store_split(n_pt, kcp_ref, kcs_ref, kc)
    _store_split(n_pt, vcp_ref, vcs_ref, vc)
    kcm_ref[...] = _bf(kc)
    vcm_ref[...] = _bf(vc)
    qd_ref[...] = h[:, 1024:1280] * (DK_D ** -0.5)
    kd_ref[...] = h[:, 1280:1536]
    vd_ref[...] = _bf(h[:, 1536:2048])
    gd_ref[...] = h[:, 2048:2560]
    z = _dot(_bf(h[:, 2560:2688]), wgu_ref[...]) + bgu_ref[...]
    la_ref[...] = (jnp.minimum(z, 0.0) - jnp.log1p(jnp.exp(-jnp.abs(z)))) * (1.0 / GATE_NORM)


def _proj_odd(x, w_in, tabc, w_gu, b_gu, t_len, n_prompt, tm):
    n = x.shape[0]
    row = lambda w: pl.BlockSpec((tm, w), lambda i: (i, 0))
    full = lambda a: pl.BlockSpec(a.shape, lambda i: (0,) * a.ndim)
    tix = _tab_index(tm, t_len, n_prompt)
    n_pt = n_prompt // tm
    pix, six = _split_rows(tm, n_pt)
    outs = [(1024, BF16), (256, BF16), (256, BF16), (256, F32), (256, F32), (512, BF16), (512, F32), (256, F32)]
    split = [KV_C * 2 * D_C, KV_C * 2 * D_C]
    return pl.pallas_call(
        functools.partial(_proj_odd_kernel, n_pt=n_pt),
        grid=(n // tm,),
        in_specs=[row(D_MODEL), full(w_in), pl.BlockSpec((tm, 3 * LANES), tix), full(w_gu), full(b_gu)],
        out_specs=[row(w) for w, _ in outs] + [pl.BlockSpec((tm, w), ix) for w in split for ix in (pix, six)],
        out_shape=[jax.ShapeDtypeStruct((n, w), d) for w, d in outs]
        + [jax.ShapeDtypeStruct((r, w), F32) for w in split for r in (n_prompt, n - n_prompt)],
        compiler_params=_params("arbitrary"),
        name="proj_odd",
    )(x, w_in, tabc, w_gu, b_gu)


def _gla_consts(c):
    n = LIN_TILE
    i = np.arange(n)
    same = (i[:, None] // c) == (i[None, :] // c)
    mats = [same & (i[None, :] <= i[:, None]),
            same & (i[None, :] > i[:, None])]
    levels = []
    s = c // 2
    while s >= 1:
        levels.append(s)
        s //= 2
    masks = []
    for s in levels:
        blk, off = i // (2 * s), i % (2 * s)
        second = off >= s
        mid = blk * 2 * s + s
        mq = second[:, None] & (i[None, :] >= mid[:, None]) & (i[None, :] <= i[:, None])
        nk = (~second)[:, None] & (i[None, :] > i[:, None]) & (i[None, :] < mid[:, None])
        mats.append(mq | nk)
        masks.append(second[:, None] & (~second)[None, :] & (blk[:, None] == blk[None, :]))
    masks.append(i[:, None] == i[None, :])
    mall = np.concatenate(mats, 0).astype(np.float32)
    return jnp.asarray(mall, BF16), jnp.asarray(np.stack(masks).astype(np.float32)), len(levels)


def _ret_consts(c):
    n = LIN_TILE
    i = np.arange(n)
    log_g = np.log1p(-np.exp2(-5.0 - np.arange(H_A, dtype=np.float64)))
    same = (i[:, None] // c) == (i[None, :] // c)
    causal = same & (i[:, None] >= i[None, :])
    dmat = np.where(causal[None], np.exp(log_g[:, None, None] * (i[:, None] - i[None, :])[None]), 0.0)
    eq = np.exp(log_g[None, :] * ((i % c) + 1)[:, None])
    ek = np.exp(log_g[None, :] * (c - 1 - (i % c))[:, None])
    rep = lambda a: np.repeat(a, DK_A, axis=1)
    tabs = np.concatenate([rep(eq), rep(ek)], 1).astype(np.float32)
    glast = [float(np.exp(log_g[h] * c)) for h in range(H_A)]
    return jnp.asarray(dmat.astype(np.float32)), jnp.asarray(tabs), glast


def _split2(x):
    hi = _bf(x)
    return hi, _bf(x - hi.astype(F32))


def _lin_tile(q, k, v, gate, la, consts, state_get, state_put, *, gla, c, dk, dv, gnorm, glast):
    n_heads = q.shape[1] // dk
    n_sub = LIN_TILE // c
    if gla:
        mall, masks, nl = consts
        hi, lo = _split2(la)
        res = _dot(mall, jnp.concatenate([hi, lo], axis=1))
        res = res[:, 0:n_heads * dk] + res[:, n_heads * dk:]
        cum = res[0:LIN_TILE]
        qe = q * jnp.exp(cum)
        kk = k * jnp.exp(res[LIN_TILE:2 * LIN_TILE])
        el = [jnp.exp(res[(2 + l) * LIN_TILE:(3 + l) * LIN_TILE]) for l in range(nl)]
        qs = [q * e for e in el] + [q]
        ks = [k * e for e in el] + [k]
    else:
        dmat, tabs = consts
        qe = q * tabs[:, 0:n_heads * dk]
        kk = k * tabs[:, n_heads * dk:2 * n_heads * dk]
    v32 = v.astype(F32) if n_sub > 1 else None
    eye = None
    outs = []
    for h in range(n_heads):
        ksl = slice(h * dk, (h + 1) * dk)
        vsl = slice(h * dv, (h + 1) * dv)
        if gla:
            scores = masks[0] * _dot_nt(_bf(qs[0][:, ksl]), _bf(ks[0][:, ksl]))
            for l in range(1, nl + 1):
                scores = scores + masks[l] * _dot_nt(_bf(qs[l][:, ksl]), _bf(ks[l][:, ksl]))
        else:
            scores = dmat[h] * _dot_nt(_bf(q[:, ksl]), _bf(k[:, ksl]))
        o = _dot(_bf(scores), v[:, vsl])
        inter = []
        for u in range(n_sub):
            rs = slice(u * c, (u + 1) * c)
            s_old = state_get(u, h)
            inter.append(_dot(_bf(qe[rs, ksl]), _bf(s_old)))
            v_u = v[:, vsl] if n_sub == 1 else _bf(v32[rs, vsl])
            upd = _dot_tn(_bf(kk[rs, ksl]), v_u)
            if gla:
                if eye is None:
                    ii = lax.broadcasted_iota(jnp.int32, (dk, dk), 0)
                    jj = lax.broadcasted_iota(jnp.int32, (dk, dk), 1)
                    eye = (ii == jj).astype(F32)
                last = jnp.exp(cum[(u + 1) * c - 1:(u + 1) * c, ksl])
                dec = jnp.sum(eye * last, axis=1, keepdims=True)
            else:
                dec = glast[h]
            state_put(u, h, dec * s_old + upd)
        o = o + (inter[0] if n_sub == 1 else jnp.concatenate(inter, 0))
        o = _rms(o)
        if gnorm is not None:
            o = o * gnorm
        outs.append(o * _silu(gate[:, vsl]))
    return outs


def _lin_kernel(*refs, gla, c, seq, n_tiles, glast):
    it = iter(refs)
    q_ref, k_ref, v_ref, g_ref = next(it), next(it), next(it), next(it)
    la_ref = next(it) if gla else None
    gn_ref = next(it) if gla else None
    c1_ref, c2_ref = next(it), next(it)
    s_in_ref = None if seq else next(it)
    o_ref, s_out_ref = next(it), next(it)
    s_scr = next(it) if seq else None
    n_heads, dk, dv = H_D, DK_D, DV_D
    n_sub = LIN_TILE // c
    consts = (c1_ref[...], c2_ref[...], int(round(math.log2(c)))) if gla else (c1_ref[...], c2_ref[...])
    gnorm = gn_ref[...] if gla else None

    if seq:
        @pl.when(pl.program_id(1) == 0)
        def _():
            s_scr[...] = jnp.zeros_like(s_scr)

    for t in range(n_tiles):
        rows = slice(t * LIN_TILE, (t + 1) * LIN_TILE)
        if seq:
            get = lambda u, h: s_scr[h]

            def put(u, h, s):
                s_scr[h] = s
        else:
            get = lambda u, h, t=t: s_in_ref[t * n_sub + u, h]

            def put(u, h, s, t=t):
                s_out_ref[t * n_sub + u, h] = s
        outs = _lin_tile(q_ref[rows, :], k_ref[rows, :], v_ref[rows, :], g_ref[rows, :],
                         la_ref[rows, :] if gla else None, consts, get, put,
                         gla=gla, c=c, dk=dk, dv=dv, gnorm=gnorm, glast=glast)
        for h in range(n_heads):
            o_ref[rows, h * dv:(h + 1) * dv] = _bf(outs[h])

    if seq:
        @pl.when(pl.program_id(1) == pl.num_programs(1) - 1)
        def _():
            s_out_ref[0] = s_scr[...]


def _lin_attn(q, k, v, gate, la, gnorm, state, layer, *, gla, n_prompt, t_len, b_prompt, c_s):
    n = q.shape[0]
    n_s = n - n_prompt
    res = []
    for seq in (True, False):
        c = LIN_TILE if seq else c_s
        if gla:
            c1, c2, _ = _gla_consts(c)
            glast = None
        else:
            c1, c2, glast = _ret_consts(c)
        rows = _tile(t_len if seq else n_s, 256)
        n_tiles = rows // LIN_TILE
        full = lambda a: pl.BlockSpec(a.shape, lambda *_: (0,) * a.ndim)
        if seq:
            per = t_len // rows
            grid = (b_prompt, per)
            rix = lambda b, i: (b * per + i, 0)
            s_shape = (b_prompt, H_D, DK_D, DV_D)
            s_spec = pl.BlockSpec((1, H_D, DK_D, DV_D), lambda b, i: (b, 0, 0, 0))
            sem = ("parallel", "arbitrary")
            scratch = [pltpu.VMEM((H_D, DK_D, DV_D), F32)]
        else:
            off = n_prompt // rows
            grid = (n_s // rows,)
            rix = lambda i: (off + i, 0)
            nb = rows // c
            s_shape = state.shape[1:]
            s_spec = pl.BlockSpec((nb, H_D, DK_D, DV_D), lambda i: (i, 0, 0, 0))
            s_in_spec = pl.BlockSpec((None, nb, H_D, DK_D, DV_D), lambda i: (layer, i, 0, 0, 0))
            sem = ("parallel",)
            scratch = []
        row = lambda w: pl.BlockSpec((rows, w), rix)
        args = [q, k, v, gate] + ([la, gnorm] if gla else []) + [c1, c2] + ([] if seq else [state])
        specs = [row(256), row(256), row(512), row(512)] + ([row(256), full(gnorm)] if gla else []) + [full(c1), full(c2)]
        specs += [] if seq else [s_in_spec]
        n_rows = n_prompt if seq else n_s
        o_rix = (lambda b, i: (b * per + i, 0)) if seq else (lambda i: (i, 0))
        o, s_out = pl.pallas_call(
            functools.partial(_lin_kernel, gla=gla, c=c, seq=seq, n_tiles=n_tiles, glast=glast),
            grid=grid,
            in_specs=specs,
            out_specs=[pl.BlockSpec((rows, 512), o_rix), s_spec],
            out_shape=[jax.ShapeDtypeStruct((n_rows, 512), BF16), jax.ShapeDtypeStruct(s_shape, F32)],
            scratch_shapes=scratch,
            compiler_params=_params(*sem),
            name=("gla" if gla else "ret") + ("_prompt" if seq else "_sample"),
        )(*args)
        res.append((o, s_out))
    return res[0][0], res[1][0], res[0][1], res[1][1]


def _col_blocks(s):
    return [s[:, c * LANES:(c + 1) * LANES] for c in range(s.shape[1] // LANES)]


def _block_max(s):
    blocks = _col_blocks(s)
    mc = blocks[0]
    for b in blocks[1:]:
        mc = jnp.maximum(mc, b)
    return jnp.max(mc, axis=-1, keepdims=True)


def _flash_update(s, m_ref, acc_ref, idx, v1):
    m_old = m_ref[idx]
    m_new = jnp.maximum(m_old, _block_max(s))
    alpha = jnp.exp2(m_old - m_new)
    p = jnp.concatenate([_bf(jnp.exp2(b - m_new)) for b in _col_blocks(s)], axis=1)
    reps = acc_ref.shape[-1] // LANES
    acc_ref[idx] = jnp.concatenate([alpha] * reps, axis=1) * acc_ref[idx] + _dot(p, v1)
    m_ref[idx] = m_new


def _causal_bias(tq, tk):
    r = lax.broadcasted_iota(jnp.int32, (tq, tk), 0)
    c = lax.broadcasted_iota(jnp.int32, (tq, tk), 1)
    return jnp.where(r >= c, 0.0, NEG_BIG).astype(F32)


def _mla_prompt_kernel(q_ref, kv_ref, wvu_ref, o_ref, m_ref, l_ref, acc_ref, *, ij=None):
    i, j = (pl.program_id(1), pl.program_id(2)) if ij is None else ij
    tq = q_ref.shape[0]
    qw = KV_LORA + LANES

    @pl.when(j == 0)
    def _():
        m_ref[...] = jnp.full_like(m_ref, NEG_BIG)
        l_ref[...] = jnp.zeros_like(l_ref)
        acc_ref[...] = jnp.zeros_like(acc_ref)

    def step(masked):
        kv = kv_ref[...]
        bias = _causal_bias(tq, tq) if masked else None
        for h in range(H_B):
            s = _dot_nt(q_ref[:, h * qw:(h + 1) * qw], kv)
            if masked:
                s = s + bias
            m_old = m_ref[h]
            m_new = jnp.maximum(m_old, _block_max(s))
            alpha = jnp.exp2(m_old - m_new)
            blocks = [jnp.exp2(b - m_new) for b in _col_blocks(s)]
            lsum = blocks[0]
            for b in blocks[1:]:
                lsum = lsum + b
            l_ref[h] = alpha * l_ref[h] + lsum
            p = jnp.concatenate([_bf(b) for b in blocks], axis=1)
            acc_ref[h] = jnp.concatenate([alpha] * (KV_LORA // LANES), axis=1) * acc_ref[h] + _dot(p, kv[:, 0:KV_LORA])
            m_ref[h] = m_new

    @pl.when(j < i)
    def _():
        step(False)

    @pl.when(j == i)
    def _():
        step(True)
        for h in range(H_B):
            lat = acc_ref[h] / jnp.sum(l_ref[h], axis=-1, keepdims=True)
            o_ref[:, h * V_B:(h + 1) * V_B] = _bf(_dot(_bf(lat), wvu_ref[h]))


def _mla_prompt(qm, kvm, w_vu, b_prompt, t_len, tq):
    n_q = t_len // tq
    qw = H_B * (KV_LORA + LANES)
    return pl.pallas_call(
        _mla_prompt_kernel,
        grid=(b_prompt, n_q, n_q),
        in_specs=[pl.BlockSpec((tq, qw), lambda b, i, j: (b * n_q + i, 0)),
                  pl.BlockSpec((tq, KV_LORA + LANES), lambda b, i, j: (b * n_q + jnp.minimum(i, j), 0)),
                  pl.BlockSpec(w_vu.shape, lambda b, i, j: (0, 0, 0))],
        out_specs=pl.BlockSpec((tq, H_B * V_B), lambda b, i, j: (b * n_q + i, 0)),
        out_shape=jax.ShapeDtypeStruct((b_prompt * t_len, H_B * V_B), BF16),
        scratch_shapes=[pltpu.VMEM((H_B, tq, LANES), F32), pltpu.VMEM((H_B, tq, LANES), F32),
                        pltpu.VMEM((H_B, tq, KV_LORA), F32)],
        compiler_params=_params("parallel", "parallel", "arbitrary"),
        name="mla_prompt",
    )(qm, kvm, w_vu)


def _lambda(lam_ref, lam_init):
    l = lam_ref[...]
    a = jnp.sum(l[0:1] * l[1:2], axis=-1, keepdims=True)
    b = jnp.sum(l[2:3] * l[3:4], axis=-1, keepdims=True)
    return jnp.exp(a) - jnp.exp(b) + lam_init


def _diff_prompt_kernel(q_ref, k_ref, v_ref, lam_ref, gsub_ref, o_ref, m_ref, acc_ref, *, lam_init, ij=None):
    i, j = (pl.program_id(1), pl.program_id(2)) if ij is None else ij
    tq = q_ref.shape[0]

    @pl.when(j == 0)
    def _():
        m_ref[...] = jnp.full_like(m_ref, NEG_BIG)
        acc_ref[...] = jnp.zeros_like(acc_ref)

    def step(masked):
        bias = _causal_bias(tq, tq) if masked else None
        ones = jnp.ones((tq, LANES), BF16)
        for g in range(KV_C):
            kg = k_ref[:, g * LANES:(g + 1) * LANES]
            v1 = jnp.concatenate([v_ref[:, g * LANES:(g + 1) * LANES], ones], axis=1)
            for rs in range(2 * REP_C):
                idx = g * 2 * REP_C + rs
                s = _dot_nt(q_ref[:, idx * LANES:(idx + 1) * LANES], kg)
                if masked:
                    s = s + bias
                _flash_update(s, m_ref, acc_ref, idx, v1)

    @pl.when(j < i)
    def _():
        step(False)

    @pl.when(j == i)
    def _():
        step(True)
        lam = _lambda(lam_ref, lam_init)
        for gr in range(KV_C * REP_C):
            a1, a2 = acc_ref[2 * gr], acc_ref[2 * gr + 1]
            o = a1[:, 0:LANES] / a1[:, LANES:2 * LANES] - lam * (a2[:, 0:LANES] / a2[:, LANES:2 * LANES])
            o_ref[:, gr * LANES:(gr + 1) * LANES] = _bf(_rms(o) * gsub_ref[...] * (1.0 - lam_init))


def _diff_prompt(qdm, kcm, vcm, lam4, g_sub, lam_init, b_prompt, t_len, tq):
    n_q = t_len // tq
    n_maps = KV_C * REP_C * 2
    return pl.pallas_call(
        functools.partial(_diff_prompt_kernel, lam_init=lam_init),
        grid=(b_prompt, n_q, n_q),
        in_specs=[pl.BlockSpec((tq, n_maps * LANES), lambda b, i, j: (b * n_q + i, 0)),
                  pl.BlockSpec((tq, KV_C * LANES), lambda b, i, j: (b * n_q + jnp.minimum(i, j), 0)),
                  pl.BlockSpec((tq, KV_C * LANES), lambda b, i, j: (b * n_q + jnp.minimum(i, j), 0)),
                  pl.BlockSpec(lam4.shape, lambda b, i, j: (0, 0)),
                  pl.BlockSpec(g_sub.shape, lambda b, i, j: (0, 0))],
        out_specs=pl.BlockSpec((tq, H_C * 2 * D_C), lambda b, i, j: (b * n_q + i, 0)),
        out_shape=jax.ShapeDtypeStruct((b_prompt * t_len, H_C * 2 * D_C), BF16),
        scratch_shapes=[pltpu.VMEM((n_maps, tq, LANES), F32), pltpu.VMEM((n_maps, tq, 2 * LANES), F32)],
        compiler_params=_params("parallel", "parallel", "arbitrary"),
        name="diff_prompt",
    )(qdm, kcm, vcm, lam4, g_sub)


def _page_copies(pt_ref, caches, bufs, sems, layer, step, slot, pg, n_groups):
    b = step // n_groups
    p0 = (step % n_groups) * pg
    out = []
    for p in range(pg):
        page = pt_ref[b, p0 + p]
        for cache, buf, sem in zip(caches, bufs, sems):
            out.append(pltpu.make_async_copy(cache.at[layer, page], buf.at[slot, p], sem.at[slot]))
    return out


def _prefetch_pages(pt_ref, caches, bufs, sems, layer, pg, n_groups, step=None, n_steps=None):
    if step is None:
        step = pl.program_id(0) * n_groups + pl.program_id(1)
        n_steps = pl.num_programs(0) * n_groups
    slot = step % 2

    @pl.when(step == 0)
    def _():
        for cp in _page_copies(pt_ref, caches, bufs, sems, layer, step, slot, pg, n_groups):
            cp.start()

    @pl.when(step + 1 < n_steps)
    def _():
        for cp in _page_copies(pt_ref, caches, bufs, sems, layer, step + 1, 1 - slot, pg, n_groups):
            cp.start()

    for cp in _page_copies(pt_ref, caches, bufs, sems, layer, step, slot, pg, n_groups):
        cp.wait()
    return slot


def _decode_update(s_parts, v_parts, tail, m_ref, l_ref, acc_ref):
    m_old = m_ref[...]
    mx = _block_max(s_parts[0])
    for sp in s_parts[1:]:
        mx = jnp.maximum(mx, _block_max(sp))
    if tail is not None:
        mx = jnp.maximum(mx, jnp.max(tail[0], axis=-1, keepdims=True))
    m_new = jnp.maximum(m_old, mx)
    alpha = jnp.exp2(m_old - m_new)
    lsum, pvs = None, [None, None]
    for idx, (sp, vp) in enumerate(zip(s_parts, v_parts)):
        blocks = [jnp.exp2(b - m_new) for b in _col_blocks(sp)]
        for b in blocks:
            lsum = b if lsum is None else lsum + b
        p = jnp.concatenate([_bf(b) for b in blocks], axis=1)
        if vp.shape[1] > LANES:
            d = jnp.concatenate([_dot(p, vp[:, c * LANES:(c + 1) * LANES]) for c in range(vp.shape[1] // LANES)], axis=1)
        else:
            d = _dot(p, vp)
        pvs[idx % 2] = d if pvs[idx % 2] is None else pvs[idx % 2] + d
    pv = pvs[0] if pvs[1] is None else pvs[0] + pvs[1]
    lrow = jnp.sum(lsum, axis=-1, keepdims=True)
    if tail is not None:
        pt = jnp.exp2(tail[0] - m_new[:, 0:1])
        lrow = lrow + jnp.sum(pt, axis=-1, keepdims=True)
        pv = pv + _dot(_bf(pt), tail[1])
    l_ref[...] = alpha * l_ref[...] + lrow
    acc_ref[...] = jnp.concatenate([alpha] * (acc_ref.shape[-1] // LANES), axis=1) * acc_ref[...] + pv
    m_ref[...] = m_new


def _mla_decode_kernel(pt_ref, q_ref, kvn_ref, wvu_ref, ckv_hbm, krt_hbm, o_ref,
                       ckv_buf, krt_buf, sem_c, sem_r, m_ref, l_ref, acc_ref, *, layer, pg, n_groups, pc, steps=None):
    gi = pl.program_id(1) if steps is None else 0
    slot = _prefetch_pages(pt_ref, (ckv_hbm, krt_hbm), (ckv_buf, krt_buf), (sem_c, sem_r), layer, pg, n_groups,
                           *(steps or ()))
    page = ckv_buf.shape[2]
    ts = q_ref.shape[0] // H_B

    @pl.when(gi == 0)
    def _():
        m_ref[...] = jnp.full_like(m_ref, NEG_BIG)
        l_ref[...] = jnp.zeros_like(l_ref)
        acc_ref[...] = jnp.zeros_like(acc_ref)

    def step(last):
        q = q_ref[...]
        q_lat, q_rope = q[:, 0:KV_LORA], q[:, KV_LORA:KV_LORA + ROPE_B]
        s_parts, v_parts = [], []
        for ch in range(pg // pc):
            kc = _bf(ckv_buf[slot, ch * pc:(ch + 1) * pc].reshape(pc * page, KV_LORA))
            krt = jnp.concatenate([_bf(krt_buf[slot, p]) for p in range(ch * pc, (ch + 1) * pc)], axis=1)
            s_parts.append(_dot_nt(q_lat, kc) + _dot(q_rope, krt))
            v_parts.append(kc)
        tail = None
        if last:
            kvn = kvn_ref[...]
            s = _dot_nt(q[:, 0:KV_LORA + ROPE_B], kvn[:, 0:KV_LORA + ROPE_B])
            r = lax.broadcasted_iota(jnp.int32, s.shape, 0) % ts
            c = lax.broadcasted_iota(jnp.int32, s.shape, 1)
            tail = (jnp.where(r >= c, s, NEG_BIG), kvn[:, 0:KV_LORA])
        _decode_update(s_parts, v_parts, tail, m_ref, l_ref, acc_ref)

    if n_groups > 1:
        @pl.when(gi < n_groups - 1)
        def _():
            step(False)

    @pl.when(gi == n_groups - 1)
    def _():
        step(True)
        lat = acc_ref[...] / l_ref[:, 0:1]
        for h in range(H_B):
            o_ref[:, h * V_B:(h + 1) * V_B] = _bf(_dot(_bf(lat[h * ts:(h + 1) * ts]), wvu_ref[h]))


def _mla_decode(page_table, q_s, kv_new, w_vu, cache_ckv, cache_kr, layer, pg, pc):
    bs, n_pages = page_table.shape
    rows = q_s.shape[1]
    ts = rows // H_B
    page = cache_ckv.shape[2]
    n_groups = n_pages // pg
    kern = functools.partial(_mla_decode_kernel, layer=layer, pg=pg, n_groups=n_groups, pc=pc)
    return pl.pallas_call(
        kern,
        grid_spec=pltpu.PrefetchScalarGridSpec(
            num_scalar_prefetch=1,
            grid=(bs, n_groups),
            in_specs=[pl.BlockSpec((None, rows, KV_LORA + LANES), lambda b, g, pt: (b, 0, 0)),
                      pl.BlockSpec((None, ts, KV_LORA + LANES), lambda b, g, pt: (b, 0, 0)),
                      pl.BlockSpec(w_vu.shape, lambda b, g, pt: (0, 0, 0)),
                      pl.BlockSpec(memory_space=pl.ANY), pl.BlockSpec(memory_space=pl.ANY)],
            out_specs=pl.BlockSpec((None, ts, H_B * V_B), lambda b, g, pt: (b, 0, 0)),
            scratch_shapes=[pltpu.VMEM((2, pg, page, KV_LORA), F32), pltpu.VMEM((2, pg, ROPE_B, page), F32),
                            pltpu.SemaphoreType.DMA((2,)), pltpu.SemaphoreType.DMA((2,)),
                            pltpu.VMEM((rows, LANES), F32), pltpu.VMEM((rows, LANES), F32),
                            pltpu.VMEM((rows, KV_LORA), F32)]),
        out_shape=jax.ShapeDtypeStruct((bs, ts, H_B * V_B), BF16),
        compiler_params=_params("arbitrary", "arbitrary"),
        name="mla_decode",
    )(page_table, q_s, kv_new, w_vu, cache_ckv, cache_kr)


def _diff_decode_kernel(pt_ref, q_ref, kn_ref, vn_ref, lam_ref, gsub_ref, k_hbm, v_hbm, o_ref,
                        k_buf, v_buf, sem_k, sem_v, m_ref, l_ref, acc_ref, *, layer, pg, n_groups, pu, pc, lam_init,
                        steps=None):
    gi = pl.program_id(1) if steps is None else 0
    slot = _prefetch_pages(pt_ref, (k_hbm, v_hbm), (k_buf, v_buf), (sem_k, sem_v), layer, pg, n_groups,
                           *(steps or ()))
    prow = k_buf.shape[2]
    rows = q_ref.shape[0]
    ts = rows // (KV_C * REP_C * 2)

    @pl.when(gi == 0)
    def _():
        m_ref[...] = jnp.full_like(m_ref, NEG_BIG)
        l_ref[...] = jnp.zeros_like(l_ref)
        acc_ref[...] = jnp.zeros_like(acc_ref)

    def group_match(n_cols):
        rg = lax.broadcasted_iota(jnp.int32, (rows, n_cols), 0) // (rows // KV_C)
        cg = lax.broadcasted_iota(jnp.int32, (rows, n_cols), 1) % KV_C
        return rg == cg

    def step(last):
        q = q_ref[...]
        same = group_match(pc * prow)
        for u in range(pg // pu):
            s_parts, v_parts = [], []
            for ch in range(u * (pu // pc), (u + 1) * (pu // pc)):
                kk = _bf(k_buf[slot, ch * pc:(ch + 1) * pc].reshape(pc * prow, LANES))
                s_parts.append(jnp.where(same, _dot_nt(q, kk), NEG_BIG))
                v_parts.append(_bf(v_buf[slot, ch * pc:(ch + 1) * pc].reshape(pc * prow, LANES)))
            tail = None
            if last and u == pg // pu - 1:
                s = _dot_nt(q, kn_ref[...])
                r = lax.broadcasted_iota(jnp.int32, s.shape, 0) % ts
                c = lax.broadcasted_iota(jnp.int32, s.shape, 1) // KV_C
                tail = (jnp.where(group_match(ts * KV_C) & (r >= c), s, NEG_BIG), vn_ref[...])
            _decode_update(s_parts, v_parts, tail, m_ref, l_ref, acc_ref)

    if n_groups > 1:
        @pl.when(gi < n_groups - 1)
        def _():
            step(False)

    @pl.when(gi == n_groups - 1)
    def _():
        step(True)
        o = acc_ref[...] / l_ref[:, 0:1]
        lam = _lambda(lam_ref, lam_init)
        for gr in range(KV_C * REP_C):
            o1 = o[(2 * gr) * ts:(2 * gr + 1) * ts]
            o2 = o[(2 * gr + 1) * ts:(2 * gr + 2) * ts]
            o_ref[:, gr * LANES:(gr + 1) * LANES] = _bf(_rms(o1 - lam * o2) * gsub_ref[...] * (1.0 - lam_init))


def _diff_decode(page_table, q_s, k_new, v_new, lam4, g_sub, cache_k, cache_v, layer, lam_init, pg, pu, pc):
    bs, n_pages = page_table.shape
    rows = q_s.shape[1]
    ts = rows // (KV_C * REP_C * 2)
    prow = cache_k.shape[2]
    n_groups = n_pages // pg
    kern = functools.partial(_diff_decode_kernel, layer=layer, pg=pg, n_groups=n_groups, pu=pu, pc=pc,
                             lam_init=lam_init)
    return pl.pallas_call(
        kern,
        grid_spec=pltpu.PrefetchScalarGridSpec(
            num_scalar_prefetch=1,
            grid=(bs, n_groups),
            in_specs=[pl.BlockSpec((None, rows, LANES), lambda b, g, pt: (b, 0, 0)),
                      pl.BlockSpec((None, ts * KV_C, LANES), lambda b, g, pt: (b, 0, 0)),
                      pl.BlockSpec((None, ts * KV_C, LANES), lambda b, g, pt: (b, 0, 0)),
                      pl.BlockSpec(lam4.shape, lambda b, g, pt: (0, 0)),
                      pl.BlockSpec(g_sub.shape, lambda b, g, pt: (0, 0)),
                      pl.BlockSpec(memory_space=pl.ANY), pl.BlockSpec(memory_space=pl.ANY)],
            out_specs=pl.BlockSpec((None, ts, H_C * 2 * D_C), lambda b, g, pt: (b, 0, 0)),
            scratch_shapes=[pltpu.VMEM((2, pg, prow, LANES), F32), pltpu.VMEM((2, pg, prow, LANES), F32),
                            pltpu.SemaphoreType.DMA((2,)), pltpu.SemaphoreType.DMA((2,)),
                            pltpu.VMEM((rows, LANES), F32), pltpu.VMEM((rows, LANES), F32),
                            pltpu.VMEM((rows, LANES), F32)]),
        out_shape=jax.ShapeDtypeStruct((bs, ts, H_C * 2 * D_C), BF16),
        compiler_params=_params("arbitrary", "arbitrary"),
        name="diff_decode",
    )(page_table, q_s, k_new, v_new, lam4, g_sub, cache_k, cache_v)


def _diff_fused_kernel(pt_ref, sc_ref, qd_ref, kn_ref, vn_ref, lam_ref, gsub_ref, qp_ref, kp_ref, vp_ref,
                       k_hbm, v_hbm, od_ref, op_ref, k_buf, v_buf, sem_k, sem_v, m_ref, l_ref, acc_ref,
                       pm_ref, pacc_ref, *, layer, pg, pu, pc, lam_init):
    s = pl.program_id(0)
    _diff_decode_kernel(pt_ref, qd_ref, kn_ref, vn_ref, lam_ref, gsub_ref, k_hbm, v_hbm, od_ref,
                        k_buf, v_buf, sem_k, sem_v, m_ref, l_ref, acc_ref, layer=layer, pg=pg, n_groups=1,
                        pu=pu, pc=pc, lam_init=lam_init, steps=(s, pl.num_programs(0)))

    @pl.when(sc_ref[3, s] == 1)
    def _():
        _diff_prompt_kernel(qp_ref, kp_ref, vp_ref, lam_ref, gsub_ref, op_ref, pm_ref, pacc_ref,
                            lam_init=lam_init, ij=(sc_ref[1, s], sc_ref[2, s]))


def _mla_fused_kernel(pt_ref, sc_ref, qd_ref, kvn_ref, wvu_ref, qp_ref, kvp_ref, ckv_hbm, krt_hbm, od_ref, op_ref,
                      ckv_buf, krt_buf, sem_c, sem_r, m_ref, l_ref, acc_ref, pm_ref, pl_ref, pacc_ref,
                      *, layer, pg, pc):
    s = pl.program_id(0)
    _mla_decode_kernel(pt_ref, qd_ref, kvn_ref, wvu_ref, ckv_hbm, krt_hbm, od_ref, ckv_buf, krt_buf, sem_c, sem_r,
                       m_ref, l_ref, acc_ref, layer=layer, pg=pg, n_groups=1, pc=pc, steps=(s, pl.num_programs(0)))

    @pl.when(sc_ref[3, s] == 1)
    def _():
        _mla_prompt_kernel(qp_ref, kvp_ref, wvu_ref, op_ref, pm_ref, pl_ref, pacc_ref, ij=(sc_ref[1, s], sc_ref[2, s]))


def _mla_fused(page_table, q_s, kv_new, w_vu, cache_ckv, cache_kr, qm, kvm, layer, pg, pc, b_prompt, t_len, tq):
    bs, n_pages = page_table.shape
    assert pg == n_pages
    rows = q_s.shape[1]
    ts = rows // H_B
    page = cache_ckv.shape[2]
    n_q = t_len // tq
    qw = KV_LORA + LANES
    sched = _tile_schedule(bs, b_prompt, n_q)
    dec = lambda s, pt, sc: (s, 0, 0)
    q_ix = lambda s, pt, sc: (sc[0, s] * n_q + sc[1, s], 0)
    kv_ix = lambda s, pt, sc: (sc[0, s] * n_q + sc[2, s], 0)
    return pl.pallas_call(
        functools.partial(_mla_fused_kernel, layer=layer, pg=pg, pc=pc),
        grid_spec=pltpu.PrefetchScalarGridSpec(
            num_scalar_prefetch=2,
            grid=(bs,),
            in_specs=[pl.BlockSpec((None, rows, qw), dec), pl.BlockSpec((None, ts, qw), dec),
                      pl.BlockSpec(w_vu.shape, lambda s, pt, sc: (0, 0, 0)),
                      pl.BlockSpec((tq, H_B * qw), q_ix), pl.BlockSpec((tq, qw), kv_ix),
                      pl.BlockSpec(memory_space=pl.ANY), pl.BlockSpec(memory_space=pl.ANY)],
            out_specs=[pl.BlockSpec((None, ts, H_B * V_B), dec), pl.BlockSpec((tq, H_B * V_B), q_ix)],
            scratch_shapes=[pltpu.VMEM((2, pg, page, KV_LORA), F32), pltpu.VMEM((2, pg, ROPE_B, page), F32),
                            pltpu.SemaphoreType.DMA((2,)), pltpu.SemaphoreType.DMA((2,)),
                            pltpu.VMEM((rows, LANES), F32), pltpu.VMEM((rows, LANES), F32),
                            pltpu.VMEM((rows, KV_LORA), F32),
                            pltpu.VMEM((H_B, tq, LANES), F32), pltpu.VMEM((H_B, tq, LANES), F32),
                            pltpu.VMEM((H_B, tq, KV_LORA), F32)]),
        out_shape=[jax.ShapeDtypeStruct((bs, ts, H_B * V_B), BF16),
                   jax.ShapeDtypeStruct((b_prompt * t_len, H_B * V_B), BF16)],
        compiler_params=pltpu.CompilerParams(dimension_semantics=("arbitrary",), vmem_limit_bytes=FUSED_VMEM_LIMIT),
        name="mla_fused",
    )(page_table, sched, q_s, kv_new, w_vu, qm, kvm, cache_ckv, cache_kr)


def _tile_schedule(n_steps, b_prompt, n_q):
    pairs = [(b, i, j) for b in range(b_prompt) for i in range(n_q) for j in range(i + 1)]
    sched = np.zeros((4, n_steps), np.int32)
    at = {(k * n_steps) // len(pairs): k for k in range(len(pairs))}
    cur = pairs[0]
    for s in range(n_steps):
        if s in at:
            cur = pairs[at[s]]
            sched[3, s] = 1
        sched[0:3, s] = cur
    return jnp.asarray(sched)


def _diff_fused(page_table, q_s, k_new, v_new, lam4, g_sub, cache_k, cache_v, qdm, kcm, vcm, layer, lam_init,
                pg, pu, pc, b_prompt, t_len, tq):
    bs, n_pages = page_table.shape
    assert pg == n_pages
    rows = q_s.shape[1]
    ts = rows // (KV_C * REP_C * 2)
    prow = cache_k.shape[2]
    n_q = t_len // tq
    n_maps = KV_C * REP_C * 2
    sched = _tile_schedule(bs, b_prompt, n_q)
    dec = lambda s, pt, sc: (s, 0, 0)
    const = lambda s, pt, sc: (0, 0)
    q_ix = lambda s, pt, sc: (sc[0, s] * n_q + sc[1, s], 0)
    kv_ix = lambda s, pt, sc: (sc[0, s] * n_q + sc[2, s], 0)
    kern = functools.partial(_diff_fused_kernel, layer=layer, pg=pg, pu=pu, pc=pc, lam_init=lam_init)
    return pl.pallas_call(
        kern,
        grid_spec=pltpu.PrefetchScalarGridSpec(
            num_scalar_prefetch=2,
            grid=(bs,),
            in_specs=[pl.BlockSpec((None, rows, LANES), dec),
                      pl.BlockSpec((None, ts * KV_C, LANES), dec),
                      pl.BlockSpec((None, ts * KV_C, LANES), dec),
                      pl.BlockSpec(lam4.shape, const), pl.BlockSpec(g_sub.shape, const),
                      pl.BlockSpec((tq, n_maps * LANES), q_ix),
                      pl.BlockSpec((tq, KV_C * LANES), kv_ix), pl.BlockSpec((tq, KV_C * LANES), kv_ix),
                      pl.BlockSpec(memory_space=pl.ANY), pl.BlockSpec(memory_space=pl.ANY)],
            out_specs=[pl.BlockSpec((None, ts, H_C * 2 * D_C), dec),
                       pl.BlockSpec((tq, H_C * 2 * D_C), q_ix)],
            scratch_shapes=[pltpu.VMEM((2, pg, prow, LANES), F32), pltpu.VMEM((2, pg, prow, LANES), F32),
                            pltpu.SemaphoreType.DMA((2,)), pltpu.SemaphoreType.DMA((2,)),
                            pltpu.VMEM((rows, LANES), F32), pltpu.VMEM((rows, LANES), F32),
                            pltpu.VMEM((rows, LANES), F32),
                            pltpu.VMEM((n_maps, tq, LANES), F32), pltpu.VMEM((n_maps, tq, 2 * LANES), F32)]),
        out_shape=[jax.ShapeDtypeStruct((bs, ts, H_C * 2 * D_C), BF16),
                   jax.ShapeDtypeStruct((b_prompt * t_len, H_C * 2 * D_C), BF16)],
        compiler_params=pltpu.CompilerParams(dimension_semantics=("arbitrary",), vmem_limit_bytes=FUSED_VMEM_LIMIT),
        name="diff_fused",
    )(page_table, sched, q_s, k_new, v_new, lam4, g_sub, qdm, kcm, vcm, cache_k, cache_v)


def _layernorm(z, g, b):
    mu = jnp.mean(z, axis=-1, keepdims=True)
    d = z - mu
    var = jnp.mean(d * d, axis=-1, keepdims=True)
    return d * lax.rsqrt(var + EPS) * g + b


def _split_rows(tm, n_pt):
    return (lambda i, *_: (jnp.minimum(i, n_pt - 1), 0)), (lambda i, *_: (jnp.maximum(i - n_pt, 0), 0))


def _out_proj_kernel(o1p_ref, o1s_ref, o2p_ref, o2s_ref, w1_ref, w2_ref, x_ref, g_ref, b_ref, y_ref, *, n_pt):
    def body(o1_ref, o2_ref):
        y = _dot(o1_ref[...], w1_ref[...]) + _dot(o2_ref[...], w2_ref[...])
        y_ref[...] = _layernorm(ALPHA * x_ref[...] + y, g_ref[...], b_ref[...])

    @pl.when(pl.program_id(0) < n_pt)
    def _():
        body(o1p_ref, o2p_ref)

    @pl.when(pl.program_id(0) >= n_pt)
    def _():
        body(o1s_ref, o2s_ref)


def _out_proj(o1p, o1s, o2p, o2s, w_out, layer, x, g, b, tm):
    n = x.shape[0]
    half = o1p.shape[1]
    n_pt = o1p.shape[0] // tm
    pix, six = _split_rows(tm, n_pt)
    row = lambda w: pl.BlockSpec((tm, w), lambda i: (i, 0))
    vec = pl.BlockSpec((1, D_MODEL), lambda i: (0, 0))
    return pl.pallas_call(
        functools.partial(_out_proj_kernel, n_pt=n_pt),
        grid=(n // tm,),
        in_specs=[pl.BlockSpec((tm, half), pix), pl.BlockSpec((tm, half), six),
                  pl.BlockSpec((tm, half), pix), pl.BlockSpec((tm, half), six),
                  pl.BlockSpec((None, half, D_MODEL), lambda i: (layer, 0, 0)),
                  pl.BlockSpec((None, half, D_MODEL), lambda i: (layer, 1, 0)), row(D_MODEL), vec, vec],
        out_specs=row(D_MODEL),
        out_shape=jax.ShapeDtypeStruct((n, D_MODEL), F32),
        compiler_params=_params("arbitrary"),
        name="out_proj_ln",
    )(o1p, o1s, o2p, o2s, w_out, w_out, x, g, b)


def _mlp_kernel(x_ref, w1_ref, w2_ref, g_ref, b_ref, *outs, n_pt, n_sub):
    sub = x_ref.shape[0] // n_sub
    ys = []
    for u in range(n_sub):
        x = x_ref[u * sub:(u + 1) * sub, :]
        h = jnp.maximum(_dot(_bf(x), w1_ref[...]), 0.0)
        y = _dot(_bf(h * h), w2_ref[...])
        ys.append(_layernorm(ALPHA * x + y, g_ref[...], b_ref[...]))
    y = ys[0] if n_sub == 1 else jnp.concatenate(ys, axis=0)
    if n_pt is None:
        outs[0][...] = y
    else:
        _store_split(n_pt, outs[0], outs[1], y)


def _mlp(x, w1, w2, layer, g, b, tm, n_prompt=None):
    n = x.shape[0]
    vec = pl.BlockSpec((1, D_MODEL), lambda i: (0, 0))
    if n_prompt is None:
        n_pt = None
        out_specs = pl.BlockSpec((tm, D_MODEL), lambda i: (i, 0))
        out_shape = jax.ShapeDtypeStruct((n, D_MODEL), F32)
    else:
        n_pt = n_prompt // tm
        pix, six = _split_rows(tm, n_pt)
        out_specs = [pl.BlockSpec((tm, D_MODEL), pix), pl.BlockSpec((tm, D_MODEL), six)]
        out_shape = [jax.ShapeDtypeStruct((n_prompt, D_MODEL), F32), jax.ShapeDtypeStruct((n - n_prompt, D_MODEL), F32)]
    once = pl.Buffered(1)
    return pl.pallas_call(
        functools.partial(_mlp_kernel, n_pt=n_pt, n_sub=2 if tm % 16 == 0 else 1),
        grid=(n // tm,),
        in_specs=[pl.BlockSpec((tm, D_MODEL), lambda i: (i, 0)),
                  pl.BlockSpec((None, D_MODEL, D_FF), lambda i: (layer, 0, 0), pipeline_mode=once),
                  pl.BlockSpec((None, D_FF, D_MODEL), lambda i: (layer, 0, 0), pipeline_mode=once), vec, vec],
        out_specs=out_specs,
        out_shape=out_shape,
        compiler_params=_params("arbitrary"),
        name="mlp_ln",
    )(x, w1, w2, g, b)


def _pad_cols(w, to):
    return jnp.pad(w, ((0, 0), (0, to - w.shape[1])))


def _prep_even(w_in, w_qu, w_ku):
    w_in_p = _bf(_pad_cols(w_in, 2304))
    nope = w_qu[:, :, :NOPE_B].reshape(Q_LORA, H_B * NOPE_B)
    rope = jnp.pad(w_qu[:, :, NOPE_B:], ((0, 0), (0, 0), (0, LANES - ROPE_B))).reshape(Q_LORA, H_B * LANES)
    w_qu_p = _bf(jnp.concatenate([nope, rope], -1))
    w_ku_t = _bf(jnp.transpose(w_ku, (1, 2, 0)))
    return w_in_p, w_qu_p, w_ku_t


def _prep_odd(w_in, w_gu):
    return _bf(_pad_cols(w_in, 2688)), _bf(jnp.pad(w_gu, ((0, LANES - GK_RANK), (0, 0))))


def kernel(x_prompt, x_sample, state_ret, cache_mla_ckv, cache_mla_krope, cache_diff_k, cache_diff_v, state_gla, page_table, w_in_even, g_q_lora, g_kv_lora, w_q_up, w_k_up, w_v_up, w_out_even, w_in_odd, lam_q1, lam_k1, lam_q2, lam_k2, g_subln, w_gate_up, b_gate_up, g_gla_norm, w_out_odd, ln1_g, ln1_b, ln2_g, ln2_b, w_ff1, w_ff2):
    bp, tp, _ = x_prompt.shape
    bs, ts, _ = x_sample.shape
    n_p, n_s = bp * tp, bs * ts
    n_pages, page = page_table.shape[1], cache_mla_ckv.shape[2]
    past_len = n_pages * page
    assert LIN_TILE % ts == 0 and n_s % LIN_TILE == 0 and tp % LIN_TILE == 0

    tm = _tile(math.gcd(tp, n_s), 512)
    tq = _tile(tp, TQ_PREF)
    pg = _tile(n_pages, 64)
    pg_mla = pg
    pu = _tile(pg, 32)
    pc = _tile(pu, 8)
    n_q = tp // tq
    fuse = pg == n_pages and bp * n_q * (n_q + 1) // 2 <= bs

    pos = np.concatenate([np.arange(tp), past_len + np.tile(np.arange(ts), bs)])
    tab_ret = _rope_table(pos, RET_THETA, DK_A, DK_A, 2)
    tab_mla = _rope_table(pos, ROPE_THETA, ROPE_B, ROPE_B, 1)
    tab_diff = _rope_table(pos, ROPE_THETA, ROT_C, D_C, 2)

    cache_krt = jnp.swapaxes(cache_mla_krope, 2, 3)
    cache_k = cache_diff_k.reshape(cache_diff_k.shape[:2] + (page * KV_C, 2 * D_C))
    cache_v = cache_diff_v.reshape(cache_diff_v.shape[:2] + (page * KV_C, 2 * D_C))

    x = jnp.concatenate([x_prompt.reshape(n_p, D_MODEL), x_sample.reshape(n_s, D_MODEL)], 0)
    row2 = lambda v: v.reshape(1, -1)
    w_ff1_b, w_ff2_b, w_out_even_b, w_out_odd_b = _bf(w_ff1), _bf(w_ff2), _bf(w_out_even), _bf(w_out_odd)
    outs = {k: [] for k in ("p_ret", "p_ckv", "p_kr", "p_dk", "p_dv", "p_gla", "s_ret", "s_ckv", "s_kr", "s_dk", "s_dv", "s_gla")}
    for l in range(DEPTH):
        j = l // 2
        if l % 2 == 0:
            w_in_p, w_qu_p, w_ku_t = _prep_even(w_in_even[j], w_q_up[j], w_k_up[j])
            w_vu = _bf(jnp.transpose(w_v_up[j], (1, 0, 2)))
            qa, ka, va, ga, qm, kvm, ckv_p, ckv_s, kr_p, kr_s = _proj_even(
                x, w_in_p, tab_ret, tab_mla, row2(g_q_lora[j]), row2(g_kv_lora[j]), w_qu_p, w_ku_t, tp, n_p, tm)
            o_ap, o_as, st_p, st_s = _lin_attn(qa, ka, va, ga, None, None, state_ret, j, gla=False, n_prompt=n_p,
                                               t_len=tp, b_prompt=bp, c_s=ts)
            qw = KV_LORA + LANES
            q_s = qm[n_p:].reshape(bs, ts, H_B, qw).transpose(0, 2, 1, 3).reshape(bs, H_B * ts, qw)
            kv_new = kvm[n_p:].reshape(bs, ts, qw)
            if fuse:
                o_bs, o_bp = _mla_fused(page_table, q_s, kv_new, w_vu, cache_mla_ckv, cache_krt, qm, kvm, j, pg_mla, pc,
                                        bp, tp, tq)
            else:
                o_bp = _mla_prompt(qm, kvm, w_vu, bp, tp, tq)
                o_bs = _mla_decode(page_table, q_s, kv_new, w_vu, cache_mla_ckv, cache_krt, j, pg_mla, pc)
            x = _out_proj(o_ap, o_as, o_bp, o_bs.reshape(n_s, H_B * V_B), w_out_even_b, j, x, row2(ln1_g[l]),
                          row2(ln1_b[l]), tm)
            outs["p_ret"].append(st_p)
            outs["s_ret"].append(st_s)
            outs["p_ckv"].append(ckv_p.reshape(bp, tp, KV_LORA))
            outs["s_ckv"].append(ckv_s.reshape(bs, ts, KV_LORA))
            outs["p_kr"].append(kr_p.reshape(bp, tp, ROPE_B))
            outs["s_kr"].append(kr_s.reshape(bs, ts, ROPE_B))
        else:
            lam_init = 0.8 - 0.6 * math.exp(-0.3 * l)
            w_in_p, w_gu_p = _prep_odd(w_in_odd[j], w_gate_up[j])
            lam4 = jnp.stack([lam_q1[j], lam_k1[j], lam_q2[j], lam_k2[j]])
            g_sub = row2(g_subln[j])
            qdm, kcm, vcm, qd, kd, vd, gd, la, kc_p, kc_s, vc_p, vc_s = _proj_odd(
                x, w_in_p, tab_diff, w_gu_p, row2(b_gate_up[j]), tp, n_p, tm)
            o_dp, o_ds, st_p, st_s = _lin_attn(qd, kd, vd, gd, la, row2(g_gla_norm[j]), state_gla, j, gla=True,
                                               n_prompt=n_p, t_len=tp, b_prompt=bp, c_s=ts)
            n_maps = KV_C * REP_C * 2
            q_s = qdm[n_p:].reshape(bs, ts, n_maps, LANES).transpose(0, 2, 1, 3).reshape(bs, n_maps * ts, LANES)
            k_new = kcm[n_p:].reshape(bs, ts * KV_C, LANES)
            v_new = vcm[n_p:].reshape(bs, ts * KV_C, LANES)
            if fuse:
                o_cs, o_cp = _diff_fused(page_table, q_s, k_new, v_new, lam4, g_sub, cache_k, cache_v, qdm, kcm, vcm,
                                         j, lam_init, pg, pu, pc, bp, tp, tq)
            else:
                o_cp = _diff_prompt(qdm, kcm, vcm, lam4, g_sub, lam_init, bp, tp, tq)
                o_cs = _diff_decode(page_table, q_s, k_new, v_new, lam4, g_sub, cache_k, cache_v, j, lam_init, pg, pu,
                                    pc)
            x = _out_proj(o_cp, o_cs.reshape(n_s, H_C * 2 * D_C), o_dp, o_ds, w_out_odd_b, j, x, row2(ln1_g[l]),
                          row2(ln1_b[l]), tm)
            outs["p_dk"].append(kc_p.reshape(bp, tp, KV_C, 2 * D_C))
            outs["s_dk"].append(kc_s.reshape(bs, ts, KV_C, 2 * D_C))
            outs["p_dv"].append(vc_p.reshape(bp, tp, KV_C, 2 * D_C))
            outs["s_dv"].append(vc_s.reshape(bs, ts, KV_C, 2 * D_C))
            outs["p_gla"].append(st_p)
            outs["s_gla"].append(st_s)
        x = _mlp(x, w_ff1_b, w_ff2_b, l, row2(ln2_g[l]), row2(ln2_b[l]), tm,
                 n_prompt=n_p if l == DEPTH - 1 else None)
    st = lambda k: jnp.stack(outs[k])
    return (x[0].reshape(bp, tp, D_MODEL), x[1].reshape(bs, ts, D_MODEL),
            st("p_ret"), st("p_ckv"), st("p_kr"), st("p_dk"), st("p_dv"), st("p_gla"),
            st("s_ret"), st("s_ckv"), st("s_kr"), st("s_dk"), st("s_dv"), st("s_gla"))
```
